```python
import jax, jax.numpy as jnp
from jax import lax
import numpy as np

D_MODEL = 1024
BATCH = 2
SEQ = 16384
DEPTH = 4

GRID_W = 64
CTX_LEN = 256
N_MIXERS = 3
D_FF = 4 * D_MODEL
EPS = 1e-6

GLA_HEADS = 4
GLA_KEY_DIM = D_MODEL // 2
GLA_VAL_DIM = D_MODEL
GLA_DK = GLA_KEY_DIM // GLA_HEADS
GLA_DV = GLA_VAL_DIM // GLA_HEADS
GLA_RANK = 16
GLA_TAU = 16.0
GLA_CHUNK = 64

CONV_WIDTH = 31
CONV_PAD = CONV_WIDTH // 2

ATTN_HEAD_DIM = 64
ATTN_Q_HEADS = D_MODEL // ATTN_HEAD_DIM
ATTN_KV_HEADS = 4
ATTN_GROUP = ATTN_Q_HEADS // ATTN_KV_HEADS
ATTN_Q_DIM = ATTN_Q_HEADS * ATTN_HEAD_DIM
ATTN_KV_DIM = ATTN_KV_HEADS * ATTN_HEAD_DIM
ATTN_BLOCK = 128
ATTN_SCALE = ATTN_HEAD_DIM ** -0.5
ROPE_THETA = 10000.0

N_GLA = (DEPTH + 2) // 3
N_CONV = (DEPTH + 1) // 3
N_ATTN = DEPTH // 3

kernel_name = "hybrid_gla_conformer_gqa_dit_trunk"


def rms_norm(x, g):
    xf = x.astype(jnp.float32)
    y = xf * lax.rsqrt(jnp.mean(xf * xf, axis=-1, keepdims=True) + EPS)
    return (y * g.astype(jnp.float32)).astype(x.dtype)


def layer_norm(x, g, b):
    xf = x.astype(jnp.float32)
    mu = jnp.mean(xf, axis=-1, keepdims=True)
    xc = xf - mu
    y = xc * lax.rsqrt(jnp.mean(xc * xc, axis=-1, keepdims=True) + EPS)
    return (y * g.astype(jnp.float32) + b.astype(jnp.float32)).astype(x.dtype)


def split_heads(t, n):
    B, L, _ = t.shape
    return t.reshape(B, L, n, -1).transpose(0, 2, 1, 3)


def axial_rope_tables(row_idx, col_idx):
    n = ATTN_HEAD_DIM // 4
    inv = ROPE_THETA ** (-jnp.arange(n, dtype=jnp.float32) / n)
    ang = jnp.concatenate([row_idx.astype(jnp.float32)[:, None] * inv,
                           col_idx.astype(jnp.float32)[:, None] * inv], axis=-1)
    return jnp.cos(ang), jnp.sin(ang)


def apply_rope(x, cos, sin):
    half = x.shape[-1] // 2
    cos = cos.astype(x.dtype)
    sin = sin.astype(x.dtype)
    x1, x2 = x[..., :half], x[..., half:]
    return jnp.concatenate([x1 * cos - x2 * sin, x2 * cos + x1 * sin], axis=-1)


def gla_chunk_scan(q, k, v, logg, s0):
    B, H, L, _ = q.shape
    n = L // GLA_CHUNK

    def to_chunks(a):
        return jnp.moveaxis(a.reshape(B, H, n, GLA_CHUNK, a.shape[-1]), 2, 0)

    mask = jnp.tril(jnp.ones((GLA_CHUNK, GLA_CHUNK), dtype=bool))

    def step(s, inp):
        qc, kc, vc, gc = inp
        b = jnp.cumsum(gc, axis=-2)
        b_last = b[..., -1:, :]
        qb = qc * jnp.exp(b)
        kb = kc * jnp.exp(-b)
        a = jnp.where(mask, jnp.einsum('bhid,bhjd->bhij', qb, kb), 0.0)
        o = jnp.einsum('bhid,bhde->bhie', qb, s) + jnp.einsum('bhij,bhje->bhie', a, vc)
        kd = kc * jnp.exp(b_last - b)
        s_new = jnp.exp(b_last[..., 0, :])[..., None] * s + jnp.einsum('bhjd,bhje->bhde', kd, vc)
        return s_new, o

    s_fin, o = lax.scan(step, s0, (to_chunks(q), to_chunks(k), to_chunks(v), to_chunks(logg)))
    o = jnp.moveaxis(o, 0, 2).reshape(B, H, L, v.shape[-1])
    return o, s_fin


def gla_mixer(h_ctx, h_lat, w_in, w_g1, w_g2, b_g, g_head, w_out, need_ctx):
    f32 = jnp.float32

    def project(h):
        z = h @ w_in
        q, k, v, r = jnp.split(z, [GLA_KEY_DIM, 2 * GLA_KEY_DIM, 2 * GLA_KEY_DIM + GLA_VAL_DIM], axis=-1)
        q = split_heads(q, GLA_HEADS).astype(f32) * (GLA_DK ** -0.5)
        k = split_heads(k, GLA_HEADS).astype(f32)
        v = split_heads(v, GLA_HEADS).astype(f32)
        gates = []
        for d in range(2):
            zg = (h @ w_g1[d]) @ w_g2[d] + b_g[d]
            gates.append(split_heads(jax.nn.log_sigmoid(zg.astype(f32)) / GLA_TAU, GLA_HEADS))
        return q, k, v, r, gates[0], gates[1]

    qc, kc, vc, rc, gcf, gcb = project(h_ctx)
    ql, kl, vl, rl, glf, glb = project(h_lat)

    def flip(a):
        return jnp.flip(a, axis=2)

    B = h_lat.shape[0]
    s0 = jnp.zeros((B, GLA_HEADS, GLA_DK, GLA_DV), f32)
    oc_f, s_f = gla_chunk_scan(qc, kc, vc, gcf, s0)
    oc_b, s_b = gla_chunk_scan(flip(qc), flip(kc), flip(vc), flip(gcb), s0)
    ol_f, _ = gla_chunk_scan(ql, kl, vl, glf, s_f)
    ol_b, _ = gla_chunk_scan(flip(ql), flip(kl), flip(vl), flip(glb), s_b)

    def finish(o, r, dtype):
        o = rms_norm(o, g_head)
        o = o.transpose(0, 2, 1, 3).reshape(o.shape[0], o.shape[2], GLA_VAL_DIM).astype(dtype)
        return (o * jax.nn.silu(r)) @ w_out

    y_lat = finish(ol_f + flip(ol_b), rl, h_lat.dtype)
    y_ctx = finish(oc_f + flip(oc_b), rc, h_ctx.dtype) if need_ctx else None
    return y_ctx, y_lat


def conformer_conv(h, w_pw1, b_pw1, w_dw, b_dw, ln_g, ln_b, w_pw2, b_pw2):
    z = h @ w_pw1 + b_pw1
    a, gt = jnp.split(z, 2, axis=-1)
    u = a * jax.nn.sigmoid(gt)
    u = lax.conv_general_dilated(u, w_dw[:, None, :], window_strides=(1,),
                                 padding=[(CONV_PAD, CONV_PAD)],
                                 dimension_numbers=('NWC', 'WIO', 'NWC'),
                                 feature_group_count=D_MODEL) + b_dw
    u = jax.nn.silu(layer_norm(u, ln_g, ln_b))
    return u @ w_pw2 + b_pw2


def sdpa(q, k, v):
    s = jnp.einsum('bkgqd,bksd->bkgqs', q, k).astype(jnp.float32) * ATTN_SCALE
    p = jax.nn.softmax(s, axis=-1).astype(v.dtype)
    return jnp.einsum('bkgqs,bksd->bkgqd', p, v)


def gqa_mixer(h_ctx, h_lat, w_qkv, g_q, g_k, w_out, cos, sin, need_ctx):
    def project(h, rope):
        B, L, _ = h.shape
        z = h @ w_qkv
        q, k, v = jnp.split(z, [ATTN_Q_DIM, ATTN_Q_DIM + ATTN_KV_DIM], axis=-1)
        q = rms_norm(q.reshape(B, L, ATTN_KV_HEADS, ATTN_GROUP, ATTN_HEAD_DIM), g_q).transpose(0, 2, 3, 1, 4)
        k = rms_norm(k.reshape(B, L, ATTN_KV_HEADS, ATTN_HEAD_DIM), g_k).transpose(0, 2, 1, 3)
        v = v.reshape(B, L, ATTN_KV_HEADS, ATTN_HEAD_DIM).transpose(0, 2, 1, 3)
        if rope:
            q = apply_rope(q, cos, sin)
            k = apply_rope(k, cos, sin)
        return q, k, v

    def merge(o):
        B, _, _, L, _ = o.shape
        return o.transpose(0, 3, 1, 2, 4).reshape(B, L, ATTN_Q_DIM) @ w_out

    qc, kc, vc = project(h_ctx, False)
    ql, kl, vl = project(h_lat, True)
    y_ctx = merge(sdpa(qc, kc, vc)) if need_ctx else None

    k_all = jnp.concatenate([kc, kl], axis=2)
    v_all = jnp.concatenate([vc, vl], axis=2)
    B, KV, G, L, hd = ql.shape
    nb = L // ATTN_BLOCK
    qb = jnp.moveaxis(ql.reshape(B, KV, G, nb, ATTN_BLOCK, hd), 3, 0)
    ob = lax.map(lambda qq: sdpa(qq, k_all, v_all), qb)
    ol = jnp.moveaxis(ob, 0, 3).reshape(B, KV, G, L, hd)
    return y_ctx, merge(ol)


def sq_relu_mlp(h, w_in, w_out):
    return jnp.square(jax.nn.relu(h @ w_in)) @ w_out


def setup_inputs(seed: int = 0) -> dict:
    key = jax.random.key(seed)
    ks = iter(jax.random.split(key, 40))
    D = D_MODEL

    def nrm(shape, scale):
        return jax.random.normal(next(ks), shape, jnp.float32) * scale

    return {
        "x": nrm((BATCH, SEQ, D), 1.0),
        "c": nrm((BATCH, D), 1.0),
        "ctx": nrm((BATCH, CTX_LEN, D), 1.0),
        "c_ctx": nrm((D,), 1.0),
        "w_mod": nrm((DEPTH, D, 6 * D), 0.5 * D ** -0.5),
        "b_mod": nrm((DEPTH, 6 * D), 0.01),
        "g_norm_mix": 1.0 + nrm((DEPTH, D), 0.02),
        "g_norm_mlp": 1.0 + nrm((DEPTH, D), 0.02),
        "w_mlp_in": nrm((DEPTH, D, D_FF), D ** -0.5),
        "w_mlp_out": nrm((DEPTH, D_FF, D), D_FF ** -0.5),
        "gla_w_in": nrm((N_GLA, D, 2 * GLA_KEY_DIM + 2 * GLA_VAL_DIM), D ** -0.5),
        "gla_w_g1": nrm((N_GLA, 2, D, GLA_RANK), D ** -0.5),
        "gla_w_g2": nrm((N_GLA, 2, GLA_RANK, GLA_KEY_DIM), GLA_RANK ** -0.5),
        "gla_b_g": nrm((N_GLA, 2, GLA_KEY_DIM), 0.1),
        "gla_g_head": 1.0 + nrm((N_GLA, GLA_DV), 0.02),
        "gla_w_out": nrm((N_GLA, GLA_VAL_DIM, D), GLA_VAL_DIM ** -0.5),
        "conv_w_pw1": nrm((N_CONV, D, 2 * D), D ** -0.5),
        "conv_b_pw1": nrm((N_CONV, 2 * D), 0.01),
        "conv_w_dw": nrm((N_CONV, CONV_WIDTH, D), CONV_WIDTH ** -0.5),
        "conv_b_dw": nrm((N_CONV, D), 0.01),
        "conv_ln_g": 1.0 + nrm((N_CONV, D), 0.02),
        "conv_ln_b": nrm((N_CONV, D), 0.01),
        "conv_w_pw2": nrm((N_CONV, D, D), D ** -0.5),
        "conv_b_pw2": nrm((N_CONV, D), 0.01),
        "attn_w_qkv": nrm((N_ATTN, D, ATTN_Q_DIM + 2 * ATTN_KV_DIM), D ** -0.5),
        "attn_g_q": 1.0 + nrm((N_ATTN, ATTN_HEAD_DIM), 0.02),
        "attn_g_k": 1.0 + nrm((N_ATTN, ATTN_HEAD_DIM), 0.02),
        "attn_w_out": nrm((N_ATTN, ATTN_Q_DIM, D), ATTN_Q_DIM ** -0.5),
    }


def reference(x, c, ctx, c_ctx, w_mod, b_mod, g_norm_mix, g_norm_mlp, w_mlp_in, w_mlp_out,
              gla_w_in, gla_w_g1, gla_w_g2, gla_b_g, gla_g_head, gla_w_out,
              conv_w_pw1, conv_b_pw1, conv_w_dw, conv_b_dw, conv_ln_g, conv_ln_b, conv_w_pw2, conv_b_pw2,
              attn_w_qkv, attn_g_q, attn_g_k, attn_w_out):
    L = x.shape[1]
    rows = L // GRID_W
    row_idx = jnp.repeat(jnp.arange(rows, dtype=jnp.int32), GRID_W)
    col_idx = jnp.tile(jnp.arange(GRID_W, dtype=jnp.int32), rows)
    cos, sin = axial_rope_tables(row_idx, col_idx)

    silu_c = jax.nn.silu(c)
    silu_cc = jax.nn.silu(c_ctx)
    x_lat, x_ctx = x, ctx
    for i in range(DEPTH):
        last = i == DEPTH - 1
        kind = i % N_MIXERS
        j = i // N_MIXERS
        mod_l = (silu_c @ w_mod[i] + b_mod[i])[:, None, :]
        mod_c = (silu_cc @ w_mod[i] + b_mod[i])[None, None, :]
        sh1_l, sc1_l, gt1_l, sh2_l, sc2_l, gt2_l = jnp.split(mod_l, 6, axis=-1)
        sh1_c, sc1_c, gt1_c, sh2_c, sc2_c, gt2_c = jnp.split(mod_c, 6, axis=-1)

        h_l = rms_norm(x_lat, g_norm_mix[i]) * (1.0 + sc1_l) + sh1_l
        h_c = rms_norm(x_ctx, g_norm_mix[i]) * (1.0 + sc1_c) + sh1_c
        if kind == 0:
            y_c, y_l = gla_mixer(h_c, h_l, gla_w_in[j], gla_w_g1[j], gla_w_g2[j], gla_b_g[j],
                                 gla_g_head[j], gla_w_out[j], not last)
        elif kind == 1:
            conv_args = (conv_w_pw1[j], conv_b_pw1[j], conv_w_dw[j], conv_b_dw[j],
                         conv_ln_g[j], conv_ln_b[j], conv_w_pw2[j], conv_b_pw2[j])
            y_l = conformer_conv(h_l, *conv_args)
            y_c = None if last else conformer_conv(h_c, *conv_args)
        else:
            y_c, y_l = gqa_mixer(h_c, h_l, attn_w_qkv[j], attn_g_q[j], attn_g_k[j], attn_w_out[j],
                                 cos, sin, not last)
        x_lat = x_lat + gt1_l * y_l

        h_l = rms_norm(x_lat, g_norm_mlp[i]) * (1.0 + sc2_l) + sh2_l
        x_lat = x_lat + gt2_l * sq_relu_mlp(h_l, w_mlp_in[i], w_mlp_out[i])
        if not last:
            x_ctx = x_ctx + gt1_c * y_c
            h_c = rms_norm(x_ctx, g_norm_mlp[i]) * (1.0 + sc2_c) + sh2_c
            x_ctx = x_ctx + gt2_c * sq_relu_mlp(h_c, w_mlp_in[i], w_mlp_out[i])
    return x_lat
```

```python
import functools

import jax
import jax.numpy as jnp
from jax import lax
from jax.experimental import pallas as pl
from jax.experimental.pallas import tpu as pltpu

F32 = jnp.float32
BF16 = jnp.bfloat16

EPS = 1e-6
GRID_W = 64
ROPE_THETA = 10000.0

GLA_HEADS = 4
GLA_TAU = 16.0
GLA_CHUNK = 64
CHUNK_SHIFT = GLA_CHUNK.bit_length() - 1
assert 1 << CHUNK_SHIFT == GLA_CHUNK

ATTN_HEAD_DIM = 64
ATTN_KV_HEADS = 4

MOD_ROWS = 8
VMEM_LIMIT = 48 * 1024 * 1024

NT_DIMS = (((1,), (1,)), ((), ()))
TN_DIMS = (((0,), (0,)), ((), ()))


def _params(*sem):
    return pltpu.CompilerParams(dimension_semantics=sem, vmem_limit_bytes=VMEM_LIMIT)


def _sigmoid(x):
    return 1.0 / (1.0 + jnp.exp(-x))


def _rms_mod(x, g, sc, sh):
    ms = jnp.mean(x * x, axis=-1, keepdims=True)
    return (x * lax.rsqrt(ms + EPS) * g) * (1.0 + sc) + sh


def _mod_spec(d, col, row_of_b):
    return pl.BlockSpec((1, 1, d), lambda b, *_: (row_of_b(b), 0, col))


def _tok_spec(tm, width, col=0):
    return pl.BlockSpec((1, tm, width), lambda b, i, *_: (b, i, col))


def _full_spec(shape):
    zeros = (0,) * len(shape)
    return pl.BlockSpec(shape, lambda *_: zeros)


def _mod_kernel(ct_ref, w_ref, b_ref, o_ref, *, rows_used):
    ct = ct_ref[...]
    s = ct * _sigmoid(ct)
    w = w_ref[0]
    bias = b_ref[0]
    rows = []
    for r in range(MOD_ROWS):
        if r < rows_used:
            rows.append(jnp.sum(w * s[:, r:r + 1], axis=0, keepdims=True) + bias)
        else:
            rows.append(jnp.zeros_like(bias))
    o_ref[0] = jnp.concatenate(rows, axis=0)


def _modulation(c, c_ctx, w_mod, b_mod):
    depth, d, n = w_mod.shape
    b = c.shape[0]
    assert b + 1 <= MOD_ROWS
    cvec = jnp.zeros((MOD_ROWS, d), F32).at[:b].set(c).at[b].set(c_ctx)
    tn = 1024
    return pl.pallas_call(
        functools.partial(_mod_kernel, rows_used=b + 1),
        grid=(depth, n // tn),
        in_specs=[_full_spec((d, MOD_ROWS)),
                  pl.BlockSpec((1, d, tn), lambda i, j: (i, 0, j)),
                  pl.BlockSpec((1, 1, tn), lambda i, j: (i, 0, j))],
        out_specs=pl.BlockSpec((1, MOD_ROWS, tn), lambda i, j: (i, 0, j)),
        out_shape=jax.ShapeDtypeStruct((depth, MOD_ROWS, n), F32),
        compiler_params=_params("parallel", "parallel"),
    )(cvec.T, w_mod, b_mod.reshape(depth, 1, n))


def _mlp_kernel(x_ref, sh_ref, sc_ref, gt_ref, g_ref, win_ref, wout_ref, o_ref, h_scr, acc_scr):
    f = pl.program_id(2)

    @pl.when(f == 0)
    def _():
        h = _rms_mod(x_ref[0], g_ref[...], sc_ref[0], sh_ref[0])
        h_scr[...] = h.astype(BF16)
        acc_scr[...] = jnp.zeros_like(acc_scr)

    a = jnp.dot(h_scr[...], win_ref[...], preferred_element_type=F32)
    a = jnp.square(jnp.maximum(a, 0.0))
    acc_scr[...] += jnp.dot(a.astype(BF16), wout_ref[...], preferred_element_type=F32)

    @pl.when(f == pl.num_programs(2) - 1)
    def _():
        o_ref[0] = x_ref[0] + gt_ref[0] * acc_scr[...]


def _mlp(x, mod3, row_of_b, g, w_in, w_out):
    b, l, d = x.shape
    dff = w_in.shape[1]
    tm = min(1024, l)
    tf = 512
    return pl.pallas_call(
        _mlp_kernel,
        grid=(b, l // tm, dff // tf),
        in_specs=[_tok_spec(tm, d),
                  _mod_spec(d, 3, row_of_b), _mod_spec(d, 4, row_of_b), _mod_spec(d, 5, row_of_b),
                  _full_spec((1, d)),
                  pl.BlockSpec((d, tf), lambda bb, i, f: (0, f)),
                  pl.BlockSpec((tf, d), lambda bb, i, f: (f, 0))],
        out_specs=_tok_spec(tm, d),
        out_shape=jax.ShapeDtypeStruct((b, l, d), F32),
        scratch_shapes=[pltpu.VMEM((tm, d), BF16), pltpu.VMEM((tm, d), F32)],
        compiler_params=_params("parallel", "parallel", "arbitrary"),
    )(x, mod3, mod3, mod3, g.reshape(1, d), w_in, w_out)


def _split2(x):
    hi = x.astype(BF16)
    lo = (x - hi.astype(F32)).astype(BF16)
    return hi, lo


def _gla_proj_kernel(x_ref, sh_ref, sc_ref, g_ref, win_ref, wg1_ref, wg2_ref, bg_ref,
                     qbf_ref, kbf_ref, kdf_ref, decf_ref, qbb_ref, kbb_ref, kdb_ref, decb_ref,
                     v_ref, r_ref, *, kdim, vdim):
    tm = x_ref.shape[1]
    h = _rms_mod(x_ref[0], g_ref[...], sc_ref[0], sh_ref[0]).astype(BF16)
    z = jnp.dot(h, win_ref[...], preferred_element_type=F32)
    q = z[:, :kdim] * ((kdim // GLA_HEADS) ** -0.5)
    k = z[:, kdim:2 * kdim]
    v_ref[0] = z[:, 2 * kdim:2 * kdim + vdim].astype(BF16)
    r_ref[0] = z[:, 2 * kdim + vdim:]

    t = jnp.dot(h, wg1_ref[...], preferred_element_type=F32)
    zg = jnp.dot(t.astype(BF16), wg2_ref[...], preferred_element_type=F32) + bg_ref[...]
    logg = (jnp.minimum(zg, 0.0) - jnp.log1p(jnp.exp(-jnp.abs(zg)))) * (1.0 / GLA_TAU)

    grp = min(tm, 256)
    row = lax.broadcasted_iota(jnp.int32, (grp, grp), 0)
    col = lax.broadcasted_iota(jnp.int32, (grp, grp), 1)
    same = (row >> CHUNK_SHIFT) == (col >> CHUNK_SHIFT)
    tri_f = (same & (col <= row)).astype(BF16)
    tri_b = (same & (col >= row)).astype(BF16)
    ones_bd = same.astype(BF16)
    nch = tm // GLA_CHUNK
    srow = lax.broadcasted_iota(jnp.int32, (8, tm), 0)
    scol = lax.broadcasted_iota(jnp.int32, (8, tm), 1)
    sel = (srow == (scol >> CHUNK_SHIFT)).astype(BF16)

    def msum(m, hi, lo):
        return (jnp.dot(m, hi, preferred_element_type=F32)
                + jnp.dot(m, lo, preferred_element_type=F32))

    outs = ((qbf_ref, kbf_ref, kdf_ref, decf_ref, tri_f), (qbb_ref, kbb_ref, kdb_ref, decb_ref, tri_b))
    for d, (qb_ref, kb_ref, kd_ref, dec_ref, tri) in enumerate(outs):
        hi, lo = _split2(logg[:, d * kdim:(d + 1) * kdim])
        dec_ref[0] = jnp.exp(msum(sel, hi, lo)[:nch])
        for gi in range(tm // grp):
            rs = slice(gi * grp, (gi + 1) * grp)
            cum = msum(tri, hi[rs], lo[rs])
            tot = msum(ones_bd, hi[rs], lo[rs])
            qb_ref[0, rs, :] = (q[rs] * jnp.exp(cum)).astype(BF16)
            kb_ref[0, rs, :] = (k[rs] * jnp.exp(-cum)).astype(BF16)
            kd_ref[0, rs, :] = (k[rs] * jnp.exp(tot - cum)).astype(BF16)


def _gla_proj(x, mod3, row_of_b, g, w_in, wg1, wg2, bg, kdim, vdim):
    b, l, d = x.shape
    tm = min(512, l)
    nc = tm // GLA_CHUNK
    tok_bf = lambda w: jax.ShapeDtypeStruct((b, l, w), BF16)
    dec = jax.ShapeDtypeStruct((b, l // GLA_CHUNK, kdim), F32)
    dec_spec = pl.BlockSpec((1, nc, kdim), lambda bb, i: (bb, i, 0))
    dir_shapes = [tok_bf(kdim), tok_bf(kdim), tok_bf(kdim), dec]
    dir_specs = [_tok_spec(tm, kdim)] * 3 + [dec_spec]
    return pl.pallas_call(
        functools.partial(_gla_proj_kernel, kdim=kdim, vdim=vdim),
        grid=(b, l // tm),
        in_specs=[_tok_spec(tm, d), _mod_spec(d, 0, row_of_b), _mod_spec(d, 1, row_of_b),
                  _full_spec((1, d)), _full_spec(w_in.shape), _full_spec(wg1.shape),
                  _full_spec(wg2.shape), _full_spec(bg.shape)],
        out_specs=dir_specs + dir_specs + [_tok_spec(tm, vdim), _tok_spec(tm, vdim)],
        out_shape=dir_shapes + dir_shapes + [tok_bf(vdim), jax.ShapeDtypeStruct((b, l, vdim), F32)],
        compiler_params=_params("parallel", "parallel"),
    )(x, mod3, mod3, g.reshape(1, d), w_in, wg1, wg2, bg)


def _gla_scan_kernel(qb_ref, kb_ref, kd_ref, v_ref, dec_ref, s0_ref, o_ref, sfin_ref, s_scr,
                     *, reverse, nchunk):
    j = pl.program_id(2)

    @pl.when(j == 0)
    def _():
        s_scr[...] = s0_ref[0, 0]

    row = lax.broadcasted_iota(jnp.int32, (GLA_CHUNK, GLA_CHUNK), 0)
    col = lax.broadcasted_iota(jnp.int32, (GLA_CHUNK, GLA_CHUNK), 1)
    mask = (col >= row) if reverse else (col <= row)
    s_t = s_scr[...]
    order = range(nchunk - 1, -1, -1) if reverse else range(nchunk)
    for c in order:
        rs = slice(c * GLA_CHUNK, (c + 1) * GLA_CHUNK)
        qb = qb_ref[0, rs, :]
        v = v_ref[0, rs, :]
        a = lax.dot_general(qb, kb_ref[0, rs, :], NT_DIMS, preferred_element_type=F32)
        a = jnp.where(mask, a, 0.0).astype(BF16)
        o = (lax.dot_general(qb, s_t.astype(BF16), NT_DIMS, preferred_element_type=F32)
             + jnp.dot(a, v, preferred_element_type=F32))
        o_ref[0, rs, :] = o
        u = lax.dot_general(v, kd_ref[0, rs, :], TN_DIMS, preferred_element_type=F32)
        s_t = dec_ref[0, c:c + 1, :] * s_t + u
    s_scr[...] = s_t

    @pl.when(j == pl.num_programs(2) - 1)
    def _():
        sfin_ref[0, 0] = s_t


def _gla_scan(qb, kb, kd, dec, v, s0, reverse):
    b, l, kdim = qb.shape
    vdim = v.shape[2]
    dk, dv = kdim // GLA_HEADS, vdim // GLA_HEADS
    tc = min(512, l)
    n = l // tc
    nchunk = tc // GLA_CHUNK
    blk = (lambda j: n - 1 - j) if reverse else (lambda j: j)
    kspec = pl.BlockSpec((1, tc, dk), lambda bb, h, j: (bb, blk(j), h))
    vspec = pl.BlockSpec((1, tc, dv), lambda bb, h, j: (bb, blk(j), h))
    sspec = pl.BlockSpec((1, 1, dv, dk), lambda bb, h, j: (bb, h, 0, 0))
    return pl.pallas_call(
        functools.partial(_gla_scan_kernel, reverse=reverse, nchunk=nchunk),
        grid=(b, GLA_HEADS, n),
        in_specs=[kspec, kspec, kspec, vspec,
                  pl.BlockSpec((1, nchunk, dk), lambda bb, h, j: (bb, blk(j), h)), sspec],
        out_specs=[vspec, sspec],
        out_shape=[jax.ShapeDtypeStruct((b, l, vdim), F32),
                   jax.ShapeDtypeStruct((b, GLA_HEADS, dv, dk), F32)],
        scratch_shapes=[pltpu.VMEM((dv, dk), F32)],
        compiler_params=_params("parallel", "parallel", "arbitrary"),
    )(qb, kb, kd, v, dec, s0)


def _gla_finish_kernel(x_ref, of_ref, ob_ref, r_ref, gt_ref, gh_ref, wout_ref, o_ref):
    o = of_ref[0] + ob_ref[0]
    dv = gh_ref.shape[1]
    parts = []
    for hd in range(o.shape[1] // dv):
        oh = o[:, hd * dv:(hd + 1) * dv]
        ms = jnp.mean(oh * oh, axis=-1, keepdims=True)
        parts.append(oh * lax.rsqrt(ms + EPS) * gh_ref[...])
    on = jnp.concatenate(parts, axis=-1)
    r = r_ref[0]
    y = on * (r * _sigmoid(r))
    o_ref[0] = x_ref[0] + gt_ref[0] * jnp.dot(y.astype(BF16), wout_ref[...], preferred_element_type=F32)


def _gla_finish(x, o_f, o_b, r, mod3, row_of_b, g_head, w_out):
    b, l, d = x.shape
    vdim = r.shape[2]
    tm = min(512, l)
    return pl.pallas_call(
        _gla_finish_kernel,
        grid=(b, l // tm),
        in_specs=[_tok_spec(tm, d), _tok_spec(tm, vdim), _tok_spec(tm, vdim), _tok_spec(tm, vdim),
                  _mod_spec(d, 2, row_of_b), _full_spec((1, g_head.shape[0])), _full_spec(w_out.shape)],
        out_specs=_tok_spec(tm, d),
        out_shape=jax.ShapeDtypeStruct((b, l, d), F32),
        compiler_params=_params("parallel", "parallel"),
    )(x, o_f, o_b, r, mod3, g_head.reshape(1, -1), w_out)


def _gla_weights(w_in, w_g1, w_g2, b_g):
    rank, kdim = w_g2.shape[1], w_g2.shape[2]
    wg1 = jnp.concatenate([w_g1[0], w_g1[1]], axis=1).astype(BF16)
    wg2 = jnp.zeros((2 * rank, 2 * kdim), F32)
    wg2 = wg2.at[:rank, :kdim].set(w_g2[0]).at[rank:, kdim:].set(w_g2[1]).astype(BF16)
    bg = jnp.concatenate([b_g[0], b_g[1]])[None, :]
    return w_in.astype(BF16), wg1, wg2, bg


def _gla_layer(x_lat, x_ctx, mod3, nb, g_mix, w_in, w_g1, w_g2, b_g, g_head, w_out, need_ctx):
    kdim = w_g2.shape[2]
    vdim = w_out.shape[0]
    w_in, wg1, wg2, bg = _gla_weights(w_in, w_g1, w_g2, b_g)
    w_out = w_out.astype(BF16)
    lat_row, ctx_row = (lambda b: b), (lambda b: nb)
    pc = _gla_proj(x_ctx, mod3, ctx_row, g_mix, w_in, wg1, wg2, bg, kdim, vdim)
    pq = _gla_proj(x_lat, mod3, lat_row, g_mix, w_in, wg1, wg2, bg, kdim, vdim)
    s0 = jnp.zeros((nb, GLA_HEADS, vdim // GLA_HEADS, kdim // GLA_HEADS), F32)
    oc_f, s_f = _gla_scan(pc[0], pc[1], pc[2], pc[3], pc[8], s0, False)
    oc_b, s_b = _gla_scan(pc[4], pc[5], pc[6], pc[7], pc[8], s0, True)
    ol_f, _ = _gla_scan(pq[0], pq[1], pq[2], pq[3], pq[8], s_f, False)
    ol_b, _ = _gla_scan(pq[4], pq[5], pq[6], pq[7], pq[8], s_b, True)
    x_lat = _gla_finish(x_lat, ol_f, ol_b, pq[9], mod3, lat_row, g_head, w_out)
    if need_ctx:
        x_ctx = _gla_finish(x_ctx, oc_f, oc_b, pc[9], mod3, ctx_row, g_head, w_out)
    return x_lat, x_ctx


def _conv_pw1_kernel(x_ref, sh_ref, sc_ref, g_ref, w_ref, b_ref, u_ref):
    d = x_ref.shape[2]
    h = _rms_mod(x_ref[0], g_ref[...], sc_ref[0], sh_ref[0]).astype(BF16)
    z = jnp.dot(h, w_ref[...], preferred_element_type=F32) + b_ref[...]
    u_ref[0] = z[:, :d] * _sigmoid(z[:, d:])


def _conv_pw1(x, mod3, row_of_b, g, w, bias):
    b, l, d = x.shape
    tm = min(512, l)
    return pl.pallas_call(
        _conv_pw1_kernel,
        grid=(b, l // tm),
        in_specs=[_tok_spec(tm, d), _mod_spec(d, 0, row_of_b), _mod_spec(d, 1, row_of_b),
                  _full_spec((1, d)), _full_spec(w.shape), _full_spec((1, 2 * d))],
        out_specs=_tok_spec(tm, d),
        out_shape=jax.ShapeDtypeStruct((b, l, d), F32),
        compiler_params=_params("parallel", "parallel"),
    )(x, mod3, mod3, g.reshape(1, d), w, bias.reshape(1, 2 * d))


CONV_HALO = 16
CONV_RB = 64


def _conv_dw_kernel(x_ref, up_ref, uc_ref, un_ref, gt_ref, wdw_ref, bdw_ref, lng_ref, lnb_ref,
                    w2_ref, b2_ref, o_ref, ubuf, cbuf, *, width):
    i = pl.program_id(1)
    tm, d = uc_ref.shape[1], uc_ref.shape[2]
    pad = width // 2
    ubuf[0:CONV_HALO, :] = jnp.where(i > 0, up_ref[0], 0.0)
    ubuf[CONV_HALO:CONV_HALO + tm, :] = uc_ref[0]
    ubuf[CONV_HALO + tm:, :] = jnp.where(i < pl.num_programs(1) - 1, un_ref[0], 0.0)
    off = CONV_HALO - pad
    for cb in range(d // 128):
        cs = slice(cb * 128, (cb + 1) * 128)
        w = wdw_ref[:, cs]
        for rb in range(tm // CONV_RB):
            base = rb * CONV_RB + off
            acc = jnp.zeros((CONV_RB, 128), F32)
            for k in range(width):
                acc = acc + w[k:k + 1, :] * ubuf[base + k:base + k + CONV_RB, cs]
            cbuf[rb * CONV_RB:(rb + 1) * CONV_RB, cs] = acc
    u = cbuf[...] + bdw_ref[...]
    mu = jnp.mean(u, axis=-1, keepdims=True)
    uc = u - mu
    y = uc * lax.rsqrt(jnp.mean(uc * uc, axis=-1, keepdims=True) + EPS) * lng_ref[...] + lnb_ref[...]
    y = y * _sigmoid(y)
    y = jnp.dot(y.astype(BF16), w2_ref[...], preferred_element_type=F32) + b2_ref[...]
    o_ref[0] = x_ref[0] + gt_ref[0] * y


def _conv_dw(x, u, mod3, row_of_b, w_dw, b_dw, ln_g, ln_b, w2, b2):
    b, l, d = x.shape
    width = w_dw.shape[0]
    assert width // 2 <= CONV_HALO
    tm = min(256, l)
    nh = tm // CONV_HALO
    last = l // CONV_HALO - 1
    wpad = jnp.zeros((-(-width // 8) * 8, d), F32).at[:width].set(w_dw)
    vec = lambda a: a.reshape(1, d)
    return pl.pallas_call(
        functools.partial(_conv_dw_kernel, width=width),
        grid=(b, l // tm),
        in_specs=[_tok_spec(tm, d),
                  pl.BlockSpec((1, CONV_HALO, d), lambda bb, i: (bb, jnp.maximum(i * nh - 1, 0), 0)),
                  _tok_spec(tm, d),
                  pl.BlockSpec((1, CONV_HALO, d), lambda bb, i: (bb, jnp.minimum((i + 1) * nh, last), 0)),
                  _mod_spec(d, 2, row_of_b), _full_spec(wpad.shape), _full_spec((1, d)),
                  _full_spec((1, d)), _full_spec((1, d)), _full_spec(w2.shape), _full_spec((1, d))],
        out_specs=_tok_spec(tm, d),
        out_shape=jax.ShapeDtypeStruct((b, l, d), F32),
        scratch_shapes=[pltpu.VMEM((tm + 2 * CONV_HALO, d), F32), pltpu.VMEM((tm, d), F32)],
        compiler_params=_params("parallel", "parallel"),
    )(x, u, u, u, mod3, wpad, vec(b_dw), vec(ln_g), vec(ln_b), w2, vec(b2))


def _conv_layer(x_lat, x_ctx, mod3, nb, g_mix, w_pw1, b_pw1, w_dw, b_dw, ln_g, ln_b, w_pw2, b_pw2, need_ctx):
    w1, w2 = w_pw1.astype(BF16), w_pw2.astype(BF16)
    streams = [(x_lat, lambda b: b)] + ([(x_ctx, lambda b: nb)] if need_ctx else [])
    outs = []
    for xs, row in streams:
        u = _conv_pw1(xs, mod3, row, g_mix, w1, b_pw1)
        outs.append(_conv_dw(xs, u, mod3, row, w_dw, b_dw, ln_g, ln_b, w2, b_pw2))
    return outs[0], (outs[1] if need_ctx else x_ctx)


def _attn_proj_kernel(x_ref, sh_ref, sc_ref, g_ref, w_ref, gq_ref, gk_ref, cos_ref, sin_ref,
                      q_ref, k_ref, v_ref, *, qdim, kvdim):
    hd = ATTN_HEAD_DIM
    h = _rms_mod(x_ref[0], g_ref[...], sc_ref[0], sh_ref[0]).astype(BF16)
    z = jnp.dot(h, w_ref[...], preferred_element_type=F32)
    row = lax.broadcasted_iota(jnp.int32, (128, 128), 0)
    col = lax.broadcasted_iota(jnp.int32, (128, 128), 1)
    shift = hd.bit_length() - 1
    head_ones = ((row >> shift) == (col >> shift)).astype(BF16)
    lane = lax.broadcasted_iota(jnp.int32, (1, 128), 1)
    upper = (lane & (hd - 1)) >= (hd // 2)
    cos = cos_ref[...]
    sin = sin_ref[...]

    def norm_rope(blk, gain):
        ssq = jnp.dot((blk * blk).astype(BF16), head_ones, preferred_element_type=F32)
        n = blk * lax.rsqrt(ssq * (1.0 / hd) + EPS) * gain
        partner = jnp.where(upper, pltpu.roll(n, hd // 2, 1), pltpu.roll(n, 128 - hd // 2, 1))
        return n * cos + partner * sin

    for j in range(qdim // 128):
        blk = norm_rope(z[:, j * 128:(j + 1) * 128], gq_ref[...]) * (hd ** -0.5)
        q_ref[0, 2 * j] = blk[:, :hd].astype(BF16)
        q_ref[0, 2 * j + 1] = blk[:, hd:].astype(BF16)
    for j in range(kvdim // 128):
        blk = norm_rope(z[:, qdim + j * 128:qdim + (j + 1) * 128], gk_ref[...])
        k_ref[0, 2 * j] = blk[:, :hd].astype(BF16)
        k_ref[0, 2 * j + 1] = blk[:, hd:].astype(BF16)
        vb = z[:, qdim + kvdim + j * 128:qdim + kvdim + (j + 1) * 128]
        v_ref[0, 2 * j] = vb[:, :hd].astype(BF16)
        v_ref[0, 2 * j + 1] = vb[:, hd:].astype(BF16)


def _attn_proj(x, mod3, row_of_b, g, w, g_q, g_k, cos, sin):
    b, l, d = x.shape
    hd = ATTN_HEAD_DIM
    kvdim = ATTN_KV_HEADS * hd
    qdim = w.shape[1] - 2 * kvdim
    tm = min(512, l)
    tile2 = lambda a: jnp.concatenate([a, a]).reshape(1, 2 * hd)
    head_spec = lambda nh: pl.BlockSpec((1, nh, tm, hd), lambda bb, i: (bb, 0, i, 0))
    head_shape = lambda nh: jax.ShapeDtypeStruct((b, nh, l, hd), BF16)
    return pl.pallas_call(
        functools.partial(_attn_proj_kernel, qdim=qdim, kvdim=kvdim),
        grid=(b, l // tm),
        in_specs=[_tok_spec(tm, d), _mod_spec(d, 0, row_of_b), _mod_spec(d, 1, row_of_b),
                  _full_spec((1, d)), _full_spec(w.shape), _full_spec((1, 2 * hd)), _full_spec((1, 2 * hd)),
                  pl.BlockSpec((tm, 128), lambda bb, i: (i, 0)), pl.BlockSpec((tm, 128), lambda bb, i: (i, 0))],
        out_specs=[head_spec(qdim // hd), head_spec(ATTN_KV_HEADS), head_spec(ATTN_KV_HEADS)],
        out_shape=[head_shape(qdim // hd), head_shape(ATTN_KV_HEADS), head_shape(ATTN_KV_HEADS)],
        compiler_params=_params("parallel", "parallel"),
    )(x, mod3, mod3, g.reshape(1, d), w, tile2(g_q), tile2(g_k), cos, sin)


def _flash_kernel(q_ref, k_ref, v_ref, o_ref, *, tk):
    grp, tq, hd = q_ref.shape[1], q_ref.shape[2], q_ref.shape[3]
    q = q_ref[0].reshape(grp * tq, hd)
    nk = k_ref.shape[2] // tk

    def body(c, carry):
        m, l, acc = carry
        start = pl.multiple_of(c * tk, tk)
        k = k_ref[0, 0, pl.ds(start, tk), :]
        v = v_ref[0, 0, pl.ds(start, tk), :]
        s = lax.dot_general(q, k, NT_DIMS, preferred_element_type=F32)
        m_new = jnp.maximum(m, jnp.max(s, axis=-1, keepdims=True))
        alpha = jnp.exp(m - m_new)
        p = jnp.exp(s - m_new)
        l = alpha * l + jnp.sum(p, axis=-1, keepdims=True)
        acc = alpha * acc + jnp.dot(p.astype(BF16), v, preferred_element_type=F32)
        return m_new, l, acc

    init = (jnp.full((grp * tq, 1), -jnp.inf, F32), jnp.zeros((grp * tq, 1), F32),
            jnp.zeros((grp * tq, hd), F32))
    _, l, acc = lax.fori_loop(0, nk, body, init)
    o = acc / l
    o_ref[0] = jnp.concatenate([o[g * tq:(g + 1) * tq] for g in range(grp)], axis=-1).astype(BF16)


def _flash(q, k, v):
    b, nq, l, hd = q.shape
    nkv, s = k.shape[1], k.shape[2]
    grp = nq // nkv
    tq = min(128, l)
    tk = 1280 if s % 1280 == 0 else 256
    assert s % tk == 0
    kv_spec = pl.BlockSpec((1, 1, s, hd), lambda bb, h, i: (bb, h, 0, 0))
    return pl.pallas_call(
        functools.partial(_flash_kernel, tk=tk),
        grid=(b, nkv, l // tq),
        in_specs=[pl.BlockSpec((1, grp, tq, hd), lambda bb, h, i: (bb, h, i, 0)), kv_spec, kv_spec],
        out_specs=pl.BlockSpec((1, tq, grp * hd), lambda bb, h, i: (bb, i, h)),
        out_shape=jax.ShapeDtypeStruct((b, l, nq * hd), BF16),
        compiler_params=_params("parallel", "parallel", "parallel"),
    )(q, k, v)


def _attn_out_kernel(x_ref, o_ref_in, gt_ref, w_ref, o_ref):
    o_ref[0] = x_ref[0] + gt_ref[0] * jnp.dot(o_ref_in[0], w_ref[...], preferred_element_type=F32)


def _attn_out(x, o, mod3, row_of_b, w):
    b, l, d = x.shape
    tm = min(512, l)
    return pl.pallas_call(
        _attn_out_kernel,
        grid=(b, l // tm),
        in_specs=[_tok_spec(tm, d), _tok_spec(tm, o.shape[2]), _mod_spec(d, 2, row_of_b), _full_spec(w.shape)],
        out_specs=_tok_spec(tm, d),
        out_shape=jax.ShapeDtypeStruct((b, l, d), F32),
        compiler_params=_params("parallel", "parallel"),
    )(x, o, mod3, w)


def _rope_tables(l):
    n = ATTN_HEAD_DIM // 4
    t = jnp.arange(l, dtype=jnp.int32)
    inv = ROPE_THETA ** (-jnp.arange(n, dtype=F32) / n)
    ang = jnp.concatenate([(t // GRID_W).astype(F32)[:, None] * inv,
                           (t % GRID_W).astype(F32)[:, None] * inv], axis=-1)
    cos, sin = jnp.cos(ang), jnp.sin(ang)
    cos_h = jnp.concatenate([cos, cos], axis=-1)
    sin_h = jnp.concatenate([-sin, sin], axis=-1)
    return jnp.concatenate([cos_h, cos_h], axis=-1), jnp.concatenate([sin_h, sin_h], axis=-1)


def _attn_layer(x_lat, x_ctx, mod3, nb, g_mix, w_qkv, g_q, g_k, w_out, need_ctx):
    l, lc = x_lat.shape[1], x_ctx.shape[1]
    w_qkv, w_out = w_qkv.astype(BF16), w_out.astype(BF16)
    lat_row, ctx_row = (lambda b: b), (lambda b: nb)
    cos, sin = _rope_tables(l)
    ql, kl, vl = _attn_proj(x_lat, mod3, lat_row, g_mix, w_qkv, g_q, g_k, cos, sin)
    qc, kc, vc = _attn_proj(x_ctx, mod3, ctx_row, g_mix, w_qkv, g_q, g_k,
                            jnp.ones((lc, 128), F32), jnp.zeros((lc, 128), F32))
    o_lat = _flash(ql, jnp.concatenate([kc, kl], axis=2), jnp.concatenate([vc, vl], axis=2))
    x_lat = _attn_out(x_lat, o_lat, mod3, lat_row, w_out)
    if need_ctx:
        x_ctx = _attn_out(x_ctx, _flash(qc, kc, vc), mod3, ctx_row, w_out)
    return x_lat, x_ctx


def kernel(x, c, ctx, c_ctx, w_mod, b_mod, g_norm_mix, g_norm_mlp, w_mlp_in, w_mlp_out,
           gla_w_in, gla_w_g1, gla_w_g2, gla_b_g, gla_g_head, gla_w_out,
           conv_w_pw1, conv_b_pw1, conv_w_dw, conv_b_dw, conv_ln_g, conv_ln_b, conv_w_pw2, conv_b_pw2,
           attn_w_qkv, attn_g_q, attn_g_k, attn_w_out):
    nb, _, d = x.shape
    depth = w_mod.shape[0]
    mod = _modulation(c, c_ctx, w_mod, b_mod)
    x_lat, x_ctx = x, ctx
    for i in range(depth):
        last = i == depth - 1
        kind, j = i % 3, i // 3
        mod3 = mod[i].reshape(MOD_ROWS, 1, 6 * d)
        if kind == 0:
            x_lat, x_ctx = _gla_layer(x_lat, x_ctx, mod3, nb, g_norm_mix[i], gla_w_in[j], gla_w_g1[j],
                                      gla_w_g2[j], gla_b_g[j], gla_g_head[j], gla_w_out[j], not last)
        elif kind == 1:
            x_lat, x_ctx = _conv_layer(x_lat, x_ctx, mod3, nb, g_norm_mix[i], conv_w_pw1[j], conv_b_pw1[j],
                                       conv_w_dw[j], conv_b_dw[j], conv_ln_g[j], conv_ln_b[j],
                                       conv_w_pw2[j], conv_b_pw2[j], not last)
        else:
            x_lat, x_ctx = _attn_layer(x_lat, x_ctx, mod3, nb, g_norm_mix[i], attn_w_qkv[j], attn_g_q[j],
                                       attn_g_k[j], attn_w_out[j], not last)
        w_in, w_out = w_mlp_in[i].astype(BF16), w_mlp_out[i].astype(BF16)
        x_lat = _mlp(x_lat, mod3, lambda b: b, g_norm_mlp[i], w_in, w_out)
        if not last:
            x_ctx = _mlp(x_ctx, mod3, lambda b: nb, g_norm_mlp[i], w_in, w_out)
    return x_lat
```

```python
import functools

import jax
import jax.numpy as jnp
from jax import lax
from jax.experimental import pallas as pl
from jax.experimental.pallas import tpu as pltpu

F32 = jnp.float32
BF16 = jnp.bfloat16

EPS = 1e-6
GRID_W = 64
ROPE_THETA = 10000.0

GLA_HEADS = 4
GLA_TAU = 16.0
GLA_CHUNK = 64
CHUNK_SHIFT = GLA_CHUNK.bit_length() - 1
assert 1 << CHUNK_SHIFT == GLA_CHUNK

ATTN_HEAD_DIM = 64
ATTN_KV_HEADS = 4
LOG2E = 1.4426950408889634
Q_PRESCALE = ATTN_HEAD_DIM ** -0.5 * LOG2E
VT_ROWS = ATTN_HEAD_DIM + 16
UNSHIFTED_SCORE_LIMIT = 40.0

MOD_ROWS = 8
VMEM_LIMIT = 48 * 1024 * 1024

NT_DIMS = (((1,), (1,)), ((), ()))
TN_DIMS = (((0,), (0,)), ((), ()))


def _params(*sem):
    return pltpu.CompilerParams(dimension_semantics=sem, vmem_limit_bytes=VMEM_LIMIT)


def _sigmoid(x):
    return 1.0 / (1.0 + jnp.exp(-x))


def _rms_mod(x, g, sc, sh):
    ms = jnp.mean(x * x, axis=-1, keepdims=True)
    return (x * lax.rsqrt(ms + EPS) * g) * (1.0 + sc) + sh


def _mod_spec(d, col, row_of_b):
    return pl.BlockSpec((1, 1, d), lambda b, *_: (row_of_b(b), 0, col))


def _tok_spec(tm, width, col=0):
    return pl.BlockSpec((1, tm, width), lambda b, i, *_: (b, i, col))


def _full_spec(shape):
    zeros = (0,) * len(shape)
    return pl.BlockSpec(shape, lambda *_: zeros)


def _mod_kernel(ct_ref, w_ref, b_ref, o_ref, *, rows_used):
    ct = ct_ref[...]
    s = ct * _sigmoid(ct)
    w = w_ref[0]
    bias = b_ref[0]
    rows = []
    for r in range(MOD_ROWS):
        if r < rows_used:
            rows.append(jnp.sum(w * s[:, r:r + 1], axis=0, keepdims=True) + bias)
        else:
            rows.append(jnp.zeros_like(bias))
    o_ref[0] = jnp.concatenate(rows, axis=0)


def _modulation(c, c_ctx, w_mod, b_mod):
    depth, d, n = w_mod.shape
    b = c.shape[0]
    assert b + 1 <= MOD_ROWS
    cvec = jnp.zeros((MOD_ROWS, d), F32).at[:b].set(c).at[b].set(c_ctx)
    tn = 1024
    return pl.pallas_call(
        functools.partial(_mod_kernel, rows_used=b + 1), name="modulation",
        grid=(depth, n // tn),
        in_specs=[_full_spec((d, MOD_ROWS)),
                  pl.BlockSpec((1, d, tn), lambda i, j: (i, 0, j)),
                  pl.BlockSpec((1, 1, tn), lambda i, j: (i, 0, j))],
        out_specs=pl.BlockSpec((1, MOD_ROWS, tn), lambda i, j: (i, 0, j)),
        out_shape=jax.ShapeDtypeStruct((depth, MOD_ROWS, n), F32),
        compiler_params=_params("parallel", "parallel"),
    )(cvec.T, w_mod, b_mod.reshape(depth, 1, n))


def _mlp_kernel(x_ref, sh_ref, sc_ref, gt_ref, g_ref, win_ref, wout_ref, o_ref, h_scr, acc_scr):
    f = pl.program_id(2)

    @pl.when(f == 0)
    def _():
        h = _rms_mod(x_ref[0], g_ref[...], sc_ref[0], sh_ref[0])
        h_scr[...] = h.astype(BF16)
        acc_scr[...] = jnp.zeros_like(acc_scr)

    a = jnp.dot(h_scr[...], win_ref[...], preferred_element_type=F32)
    a = jnp.square(jnp.maximum(a, 0.0))
    acc_scr[...] += jnp.dot(a.astype(BF16), wout_ref[...], preferred_element_type=F32)

    @pl.when(f == pl.num_programs(2) - 1)
    def _():
        o_ref[0] = x_ref[0] + gt_ref[0] * acc_scr[...]


def _mlp(x, mod3, row_of_b, g, w_in, w_out):
    b, l, d = x.shape
    dff = w_in.shape[1]
    tm = min(1024, l)
    tf = 512
    return pl.pallas_call(
        _mlp_kernel, name="mlp",
        grid=(b, l // tm, dff // tf),
        in_specs=[_tok_spec(tm, d),
                  _mod_spec(d, 3, row_of_b), _mod_spec(d, 4, row_of_b), _mod_spec(d, 5, row_of_b),
                  _full_spec((1, d)),
                  pl.BlockSpec((d, tf), lambda bb, i, f: (0, f)),
                  pl.BlockSpec((tf, d), lambda bb, i, f: (f, 0))],
        out_specs=_tok_spec(tm, d),
        out_shape=jax.ShapeDtypeStruct((b, l, d), F32),
        scratch_shapes=[pltpu.VMEM((tm, d), BF16), pltpu.VMEM((tm, d), F32)],
        compiler_params=_params("parallel", "parallel", "arbitrary"),
    )(x, mod3, mod3, mod3, g.reshape(1, d), w_in, w_out)


def _split2(x):
    hi = x.astype(BF16)
    lo = (x - hi.astype(F32)).astype(BF16)
    return hi, lo


def _gla_proj_kernel(x_ref, sh_ref, sc_ref, g_ref, win_ref, wg1_ref, wg2_ref, bg_ref,
                     qbf_ref, kbf_ref, kdf_ref, decf_ref, qbb_ref, kbb_ref, kdb_ref, decb_ref,
                     v_ref, r_ref, *, kdim, vdim):
    tm = x_ref.shape[1]
    h = _rms_mod(x_ref[0], g_ref[...], sc_ref[0], sh_ref[0]).astype(BF16)
    z = jnp.dot(h, win_ref[...], preferred_element_type=F32)
    q = z[:, :kdim] * ((kdim // GLA_HEADS) ** -0.5)
    k = z[:, kdim:2 * kdim]
    v_ref[0] = z[:, 2 * kdim:2 * kdim + vdim].astype(BF16)
    r_ref[0] = z[:, 2 * kdim + vdim:]

    t = jnp.dot(h, wg1_ref[...], preferred_element_type=F32)
    zg = jnp.dot(t.astype(BF16), wg2_ref[...], preferred_element_type=F32) + bg_ref[...]
    logg = (jnp.minimum(zg, 0.0) - jnp.log1p(jnp.exp(-jnp.abs(zg)))) * (1.0 / GLA_TAU)

    grp = min(tm, 256)
    row = lax.broadcasted_iota(jnp.int32, (grp, grp), 0)
    col = lax.broadcasted_iota(jnp.int32, (grp, grp), 1)
    same = (row >> CHUNK_SHIFT) == (col >> CHUNK_SHIFT)
    tri_f = (same & (col <= row)).astype(BF16)
    tri_b = (same & (col >= row)).astype(BF16)
    ones_bd = same.astype(BF16)
    nch = tm // GLA_CHUNK
    srow = lax.broadcasted_iota(jnp.int32, (8, tm), 0)
    scol = lax.broadcasted_iota(jnp.int32, (8, tm), 1)
    sel = (srow == (scol >> CHUNK_SHIFT)).astype(BF16)

    def msum(m, hi, lo):
        return (jnp.dot(m, hi, preferred_element_type=F32)
                + jnp.dot(m, lo, preferred_element_type=F32))

    outs = ((qbf_ref, kbf_ref, kdf_ref, decf_ref, tri_f), (qbb_ref, kbb_ref, kdb_ref, decb_ref, tri_b))
    for d, (qb_ref, kb_ref, kd_ref, dec_ref, tri) in enumerate(outs):
        hi, lo = _split2(logg[:, d * kdim:(d + 1) * kdim])
        dec_ref[0] = jnp.exp(msum(sel, hi, lo)[:nch])
        for gi in range(tm // grp):
            rs = slice(gi * grp, (gi + 1) * grp)
            cum = msum(tri, hi[rs], lo[rs])
            tot = msum(ones_bd, hi[rs], lo[rs])
            qb_ref[0, rs, :] = (q[rs] * jnp.exp(cum)).astype(BF16)
            kb_ref[0, rs, :] = (k[rs] * jnp.exp(-cum)).astype(BF16)
            kd_ref[0, rs, :] = (k[rs] * jnp.exp(tot - cum)).astype(BF16)


def _gla_proj(x, mod3, row_of_b, g, w_in, wg1, wg2, bg, kdim, vdim):
    b, l, d = x.shape
    tm = min(512, l)
    nc = tm // GLA_CHUNK
    tok_bf = lambda w: jax.ShapeDtypeStruct((b, l, w), BF16)
    dec = jax.ShapeDtypeStruct((b, l // GLA_CHUNK, kdim), F32)
    dec_spec = pl.BlockSpec((1, nc, kdim), lambda bb, i: (bb, i, 0))
    dir_shapes = [tok_bf(kdim), tok_bf(kdim), tok_bf(kdim), dec]
    dir_specs = [_tok_spec(tm, kdim)] * 3 + [dec_spec]
    return pl.pallas_call(
        functools.partial(_gla_proj_kernel, kdim=kdim, vdim=vdim), name="gla_proj",
        grid=(b, l // tm),
        in_specs=[_tok_spec(tm, d), _mod_spec(d, 0, row_of_b), _mod_spec(d, 1, row_of_b),
                  _full_spec((1, d)), _full_spec(w_in.shape), _full_spec(wg1.shape),
                  _full_spec(wg2.shape), _full_spec(bg.shape)],
        out_specs=dir_specs + dir_specs + [_tok_spec(tm, vdim), _tok_spec(tm, vdim)],
        out_shape=dir_shapes + dir_shapes + [tok_bf(vdim), jax.ShapeDtypeStruct((b, l, vdim), F32)],
        compiler_params=_params("parallel", "parallel"),
    )(x, mod3, mod3, g.reshape(1, d), w_in, wg1, wg2, bg)


def _gla_scan_kernel(qb_ref, kb_ref, kd_ref, v_ref, dec_ref, s0_ref, o_ref, sfin_ref, s_scr,
                     *, reverse, nchunk):
    j = pl.program_id(2)

    @pl.when(j == 0)
    def _():
        s_scr[...] = s0_ref[0, 0]

    row = lax.broadcasted_iota(jnp.int32, (GLA_CHUNK, GLA_CHUNK), 0)
    col = lax.broadcasted_iota(jnp.int32, (GLA_CHUNK, GLA_CHUNK), 1)
    mask = (col >= row) if reverse else (col <= row)
    s_t = s_scr[...]
    order = range(nchunk - 1, -1, -1) if reverse else range(nchunk)
    for c in order:
        rs = slice(c * GLA_CHUNK, (c + 1) * GLA_CHUNK)
        qb = qb_ref[0, rs, :]
        v = v_ref[0, rs, :]
        a = lax.dot_general(qb, kb_ref[0, rs, :], NT_DIMS, preferred_element_type=F32)
        a = jnp.where(mask, a, 0.0).astype(BF16)
        o = (lax.dot_general(qb, s_t.astype(BF16), NT_DIMS, preferred_element_type=F32)
             + jnp.dot(a, v, preferred_element_type=F32))
        o_ref[0, rs, :] = o
        u = lax.dot_general(v, kd_ref[0, rs, :], TN_DIMS, preferred_element_type=F32)
        s_t = dec_ref[0, c:c + 1, :] * s_t + u
    s_scr[...] = s_t

    @pl.when(j == pl.num_programs(2) - 1)
    def _():
        sfin_ref[0, 0] = s_t


def _gla_scan(qb, kb, kd, dec, v, s0, reverse):
    b, l, kdim = qb.shape
    vdim = v.shape[2]
    dk, dv = kdim // GLA_HEADS, vdim // GLA_HEADS
    tc = min(512, l)
    n = l // tc
    nchunk = tc // GLA_CHUNK
    blk = (lambda j: n - 1 - j) if reverse else (lambda j: j)
    kspec = pl.BlockSpec((1, tc, dk), lambda bb, h, j: (bb, blk(j), h))
    vspec = pl.BlockSpec((1, tc, dv), lambda bb, h, j: (bb, blk(j), h))
    sspec = pl.BlockSpec((1, 1, dv, dk), lambda bb, h, j: (bb, h, 0, 0))
    return pl.pallas_call(
        functools.partial(_gla_scan_kernel, reverse=reverse, nchunk=nchunk), name="gla_scan",
        grid=(b, GLA_HEADS, n),
        in_specs=[kspec, kspec, kspec, vspec,
                  pl.BlockSpec((1, nchunk, dk), lambda bb, h, j: (bb, blk(j), h)), sspec],
        out_specs=[vspec, sspec],
        out_shape=[jax.ShapeDtypeStruct((b, l, vdim), F32),
                   jax.ShapeDtypeStruct((b, GLA_HEADS, dv, dk), F32)],
        scratch_shapes=[pltpu.VMEM((dv, dk), F32)],
        compiler_params=_params("parallel", "parallel", "arbitrary"),
    )(qb, kb, kd, v, dec, s0)


def _gla_finish_kernel(x_ref, of_ref, ob_ref, r_ref, gt_ref, gh_ref, wout_ref, o_ref):
    o = of_ref[0] + ob_ref[0]
    dv = gh_ref.shape[1]
    parts = []
    for hd in range(o.shape[1] // dv):
        oh = o[:, hd * dv:(hd + 1) * dv]
        ms = jnp.mean(oh * oh, axis=-1, keepdims=True)
        parts.append(oh * lax.rsqrt(ms + EPS) * gh_ref[...])
    on = jnp.concatenate(parts, axis=-1)
    r = r_ref[0]
    y = on * (r * _sigmoid(r))
    o_ref[0] = x_ref[0] + gt_ref[0] * jnp.dot(y.astype(BF16), wout_ref[...], preferred_element_type=F32)


def _gla_finish(x, o_f, o_b, r, mod3, row_of_b, g_head, w_out):
    b, l, d = x.shape
    vdim = r.shape[2]
    tm = min(512, l)
    return pl.pallas_call(
        _gla_finish_kernel, name="gla_finish",
        grid=(b, l // tm),
        in_specs=[_tok_spec(tm, d), _tok_spec(tm, vdim), _tok_spec(tm, vdim), _tok_spec(tm, vdim),
                  _mod_spec(d, 2, row_of_b), _full_spec((1, g_head.shape[0])), _full_spec(w_out.shape)],
        out_specs=_tok_spec(tm, d),
        out_shape=jax.ShapeDtypeStruct((b, l, d), F32),
        compiler_params=_params("parallel", "parallel"),
    )(x, o_f, o_b, r, mod3, g_head.reshape(1, -1), w_out)


def _gla_weights(w_in, w_g1, w_g2, b_g):
    rank, kdim = w_g2.shape[1], w_g2.shape[2]
    wg1 = jnp.concatenate([w_g1[0], w_g1[1]], axis=1).astype(BF16)
    wg2 = jnp.zeros((2 * rank, 2 * kdim), F32)
    wg2 = wg2.at[:rank, :kdim].set(w_g2[0]).at[rank:, kdim:].set(w_g2[1]).astype(BF16)
    bg = jnp.concatenate([b_g[0], b_g[1]])[None, :]
    return w_in.astype(BF16), wg1, wg2, bg


def _gla_layer(x_lat, x_ctx, mod3, nb, g_mix, w_in, w_g1, w_g2, b_g, g_head, w_out, need_ctx):
    kdim = w_g2.shape[2]
    vdim = w_out.shape[0]
    w_in, wg1, wg2, bg = _gla_weights(w_in, w_g1, w_g2, b_g)
    w_out = w_out.astype(BF16)
    lat_row, ctx_row = (lambda b: b), (lambda b: nb)
    pc = _gla_proj(x_ctx, mod3, ctx_row, g_mix, w_in, wg1, wg2, bg, kdim, vdim)
    pq = _gla_proj(x_lat, mod3, lat_row, g_mix, w_in, wg1, wg2, bg, kdim, vdim)
    s0 = jnp.zeros((nb, GLA_HEADS, vdim // GLA_HEADS, kdim // GLA_HEADS), F32)
    oc_f, s_f = _gla_scan(pc[0], pc[1], pc[2], pc[3], pc[8], s0, False)
    oc_b, s_b = _gla_scan(pc[4], pc[5], pc[6], pc[7], pc[8], s0, True)
    ol_f, _ = _gla_scan(pq[0], pq[1], pq[2], pq[3], pq[8], s_f, False)
    ol_b, _ = _gla_scan(pq[4], pq[5], pq[6], pq[7], pq[8], s_b, True)
    x_lat = _gla_finish(x_lat, ol_f, ol_b, pq[9], mod3, lat_row, g_head, w_out)
    if need_ctx:
        x_ctx = _gla_finish(x_ctx, oc_f, oc_b, pc[9], mod3, ctx_row, g_head, w_out)
    return x_lat, x_ctx


def _conv_pw1_kernel(x_ref, sh_ref, sc_ref, g_ref, w_ref, b_ref, u_ref):
    d = x_ref.shape[2]
    h = _rms_mod(x_ref[0], g_ref[...], sc_ref[0], sh_ref[0]).astype(BF16)
    z = jnp.dot(h, w_ref[...], preferred_element_type=F32) + b_ref[...]
    u_ref[0] = z[:, :d] * _sigmoid(z[:, d:])


def _conv_pw1(x, mod3, row_of_b, g, w, bias):
    b, l, d = x.shape
    tm = min(512, l)
    return pl.pallas_call(
        _conv_pw1_kernel, name="conv_pw1",
        grid=(b, l // tm),
        in_specs=[_tok_spec(tm, d), _mod_spec(d, 0, row_of_b), _mod_spec(d, 1, row_of_b),
                  _full_spec((1, d)), _full_spec(w.shape), _full_spec((1, 2 * d))],
        out_specs=_tok_spec(tm, d),
        out_shape=jax.ShapeDtypeStruct((b, l, d), F32),
        compiler_params=_params("parallel", "parallel"),
    )(x, mod3, mod3, g.reshape(1, d), w, bias.reshape(1, 2 * d))


CONV_HALO = 16
CONV_RB = 64


def _conv_dw_kernel(x_ref, up_ref, uc_ref, un_ref, gt_ref, wdw_ref, bdw_ref, lng_ref, lnb_ref,
                    w2_ref, b2_ref, o_ref, ubuf, cbuf, *, width):
    i = pl.program_id(1)
    tm, d = uc_ref.shape[1], uc_ref.shape[2]
    pad = width // 2
    ubuf[0:CONV_HALO, :] = jnp.where(i > 0, up_ref[0], 0.0)
    ubuf[CONV_HALO:CONV_HALO + tm, :] = uc_ref[0]
    ubuf[CONV_HALO + tm:, :] = jnp.where(i < pl.num_programs(1) - 1, un_ref[0], 0.0)
    off = CONV_HALO - pad
    for cb in range(d // 128):
        cs = slice(cb * 128, (cb + 1) * 128)
        w = wdw_ref[:, cs]
        for rb in range(tm // CONV_RB):
            base = rb * CONV_RB + off
            acc = jnp.zeros((CONV_RB, 128), F32)
            for k in range(width):
                acc = acc + w[k:k + 1, :] * ubuf[base + k:base + k + CONV_RB, cs]
            cbuf[rb * CONV_RB:(rb + 1) * CONV_RB, cs] = acc
    u = cbuf[...] + bdw_ref[...]
    mu = jnp.mean(u, axis=-1, keepdims=True)
    uc = u - mu
    y = uc * lax.rsqrt(jnp.mean(uc * uc, axis=-1, keepdims=True) + EPS) * lng_ref[...] + lnb_ref[...]
    y = y * _sigmoid(y)
    y = jnp.dot(y.astype(BF16), w2_ref[...], preferred_element_type=F32) + b2_ref[...]
    o_ref[0] = x_ref[0] + gt_ref[0] * y


def _conv_dw(x, u, mod3, row_of_b, w_dw, b_dw, ln_g, ln_b, w2, b2):
    b, l, d = x.shape
    width = w_dw.shape[0]
    assert width // 2 <= CONV_HALO
    tm = min(256, l)
    nh = tm // CONV_HALO
    last = l // CONV_HALO - 1
    wpad = jnp.zeros((-(-width // 8) * 8, d), F32).at[:width].set(w_dw)
    vec = lambda a: a.reshape(1, d)
    return pl.pallas_call(
        functools.partial(_conv_dw_kernel, width=width), name="conv_dw",
        grid=(b, l // tm),
        in_specs=[_tok_spec(tm, d),
                  pl.BlockSpec((1, CONV_HALO, d), lambda bb, i: (bb, jnp.maximum(i * nh - 1, 0), 0)),
                  _tok_spec(tm, d),
                  pl.BlockSpec((1, CONV_HALO, d), lambda bb, i: (bb, jnp.minimum((i + 1) * nh, last), 0)),
                  _mod_spec(d, 2, row_of_b), _full_spec(wpad.shape), _full_spec((1, d)),
                  _full_spec((1, d)), _full_spec((1, d)), _full_spec(w2.shape), _full_spec((1, d))],
        out_specs=_tok_spec(tm, d),
        out_shape=jax.ShapeDtypeStruct((b, l, d), F32),
        scratch_shapes=[pltpu.VMEM((tm + 2 * CONV_HALO, d), F32), pltpu.VMEM((tm, d), F32)],
        compiler_params=_params("parallel", "parallel"),
    )(x, u, u, u, mod3, wpad, vec(b_dw), vec(ln_g), vec(ln_b), w2, vec(b2))


def _conv_layer(x_lat, x_ctx, mod3, nb, g_mix, w_pw1, b_pw1, w_dw, b_dw, ln_g, ln_b, w_pw2, b_pw2, need_ctx):
    w1, w2 = w_pw1.astype(BF16), w_pw2.astype(BF16)
    streams = [(x_lat, lambda b: b)] + ([(x_ctx, lambda b: nb)] if need_ctx else [])
    outs = []
    for xs, row in streams:
        u = _conv_pw1(xs, mod3, row, g_mix, w1, b_pw1)
        outs.append(_conv_dw(xs, u, mod3, row, w_dw, b_dw, ln_g, ln_b, w2, b_pw2))
    return outs[0], (outs[1] if need_ctx else x_ctx)


def _attn_proj_kernel(x_ref, sh_ref, sc_ref, g_ref, w_ref, gq_ref, gk_ref, cos_ref, sin_ref,
                      q_ref, k_ref, vt_ref, *, qdim, kvdim):
    hd = ATTN_HEAD_DIM
    h = _rms_mod(x_ref[0], g_ref[...], sc_ref[0], sh_ref[0]).astype(BF16)
    z = jnp.dot(h, w_ref[...], preferred_element_type=F32)
    row = lax.broadcasted_iota(jnp.int32, (128, 128), 0)
    col = lax.broadcasted_iota(jnp.int32, (128, 128), 1)
    shift = hd.bit_length() - 1
    head_ones = ((row >> shift) == (col >> shift)).astype(BF16)
    lane = lax.broadcasted_iota(jnp.int32, (1, 128), 1)
    upper = (lane & (hd - 1)) >= (hd // 2)
    cos = cos_ref[...]
    sin = sin_ref[...]

    def norm_rope(blk, gain):
        ssq = jnp.dot((blk * blk).astype(BF16), head_ones, preferred_element_type=F32)
        n = blk * lax.rsqrt(ssq * (1.0 / hd) + EPS) * gain
        partner = jnp.where(upper, pltpu.roll(n, hd // 2, 1), pltpu.roll(n, 128 - hd // 2, 1))
        return n * cos + partner * sin

    tm = x_ref.shape[1]
    ones_rows = (lax.broadcasted_iota(jnp.int32, (VT_ROWS - hd, tm), 0) == 0).astype(BF16)
    for j in range(qdim // 128):
        blk = norm_rope(z[:, j * 128:(j + 1) * 128], gq_ref[...]) * Q_PRESCALE
        q_ref[0, 2 * j] = blk[:, :hd].astype(BF16)
        q_ref[0, 2 * j + 1] = blk[:, hd:].astype(BF16)
    for j in range(kvdim // 128):
        blk = norm_rope(z[:, qdim + j * 128:qdim + (j + 1) * 128], gk_ref[...])
        k_ref[0, 2 * j] = blk[:, :hd].astype(BF16)
        k_ref[0, 2 * j + 1] = blk[:, hd:].astype(BF16)
        vbt = z[:, qdim + kvdim + j * 128:qdim + kvdim + (j + 1) * 128].T
        for half in range(2):
            vt_ref[0, 2 * j + half, 0:hd, :] = vbt[half * hd:(half + 1) * hd].astype(BF16)
            vt_ref[0, 2 * j + half, hd:VT_ROWS, :] = ones_rows


def _attn_proj(x, mod3, row_of_b, g, w, g_q, g_k, cos, sin):
    b, l, d = x.shape
    hd = ATTN_HEAD_DIM
    kvdim = ATTN_KV_HEADS * hd
    qdim = w.shape[1] - 2 * kvdim
    tm = min(512, l)
    tile2 = lambda a: jnp.concatenate([a, a]).reshape(1, 2 * hd)
    head_spec = lambda nh: pl.BlockSpec((1, nh, tm, hd), lambda bb, i: (bb, 0, i, 0))
    head_shape = lambda nh: jax.ShapeDtypeStruct((b, nh, l, hd), BF16)
    return pl.pallas_call(
        functools.partial(_attn_proj_kernel, qdim=qdim, kvdim=kvdim), name="attn_proj",
        grid=(b, l // tm),
        in_specs=[_tok_spec(tm, d), _mod_spec(d, 0, row_of_b), _mod_spec(d, 1, row_of_b),
                  _full_spec((1, d)), _full_spec(w.shape), _full_spec((1, 2 * hd)), _full_spec((1, 2 * hd)),
                  pl.BlockSpec((tm, 128), lambda bb, i: (i, 0)), pl.BlockSpec((tm, 128), lambda bb, i: (i, 0))],
        out_specs=[head_spec(qdim // hd), head_spec(ATTN_KV_HEADS),
                   pl.BlockSpec((1, ATTN_KV_HEADS, VT_ROWS, tm), lambda bb, i: (bb, 0, 0, i))],
        out_shape=[head_shape(qdim // hd), head_shape(ATTN_KV_HEADS),
                   jax.ShapeDtypeStruct((b, ATTN_KV_HEADS, VT_ROWS, l), BF16)],
        compiler_params=_params("parallel", "parallel"),
    )(x, mod3, mod3, g.reshape(1, d), w, tile2(g_q), tile2(g_k), cos, sin)


def _flash_kernel(unshifted_ref, q_ref, k_ref, vt_ref, o_ref):
    grp, tq, hd = q_ref.shape[1], q_ref.shape[2], q_ref.shape[3]
    rows = grp * tq
    q = q_ref[0].reshape(rows, hd)
    nk, tk = vt_ref.shape[2], vt_ref.shape[4]

    def scores_t(c):
        start = pl.multiple_of(c * tk, tk)
        return lax.dot_general(k_ref[0, 0, pl.ds(start, tk), :], q, NT_DIMS, preferred_element_type=F32)

    def finish(acc):
        o_t = acc[:hd] / acc[hd:hd + 1]
        o_ref[0] = jnp.concatenate([o_t[:, g * tq:(g + 1) * tq].T for g in range(grp)],
                                   axis=-1).astype(BF16)

    @pl.when(unshifted_ref[0] == 1)
    def _():
        def body(c, acc):
            p_t = jnp.exp2(scores_t(c)).astype(BF16)
            return acc + jnp.dot(vt_ref[0, 0, c], p_t, preferred_element_type=F32)

        finish(lax.fori_loop(0, nk, body, jnp.zeros((VT_ROWS, rows), F32), unroll=True))

    @pl.when(unshifted_ref[0] != 1)
    def _():
        def body(c, carry):
            m, acc = carry
            s_t = scores_t(c)
            m_new = jnp.maximum(m, jnp.max(s_t, axis=0, keepdims=True))
            p_t = jnp.exp2(s_t - m_new).astype(BF16)
            acc = jnp.exp2(m - m_new) * acc + jnp.dot(vt_ref[0, 0, c], p_t, preferred_element_type=F32)
            return m_new, acc

        init = (jnp.full((1, rows), -jnp.inf, F32), jnp.zeros((VT_ROWS, rows), F32))
        finish(lax.fori_loop(0, nk, body, init)[1])


def _flash(q, k, vt, unshifted):
    b, nq, l, hd = q.shape
    nkv, s = k.shape[1], k.shape[2]
    grp = nq // nkv
    tq = min(128, l)
    tk = 1280 if s % 1280 == 0 else 256
    assert s % tk == 0
    vt = vt.reshape(b, nkv, VT_ROWS, s // tk, tk).transpose(0, 1, 3, 2, 4)
    return pl.pallas_call(
        _flash_kernel, name="flash",
        grid_spec=pltpu.PrefetchScalarGridSpec(
            num_scalar_prefetch=1,
            grid=(b, nkv, l // tq),
            in_specs=[pl.BlockSpec((1, grp, tq, hd), lambda bb, h, i, *_: (bb, h, i, 0)),
                      pl.BlockSpec((1, 1, s, hd), lambda bb, h, i, *_: (bb, h, 0, 0)),
                      pl.BlockSpec((1, 1, s // tk, VT_ROWS, tk), lambda bb, h, i, *_: (bb, h, 0, 0, 0))],
            out_specs=pl.BlockSpec((1, tq, grp * hd), lambda bb, h, i, *_: (bb, i, h))),
        out_shape=jax.ShapeDtypeStruct((b, l, nq * hd), BF16),
        compiler_params=_params("parallel", "parallel", "parallel"),
    )(unshifted, q, k, vt)


def _attn_out_kernel(x_ref, o_ref_in, gt_ref, w_ref, o_ref):
    o_ref[0] = x_ref[0] + gt_ref[0] * jnp.dot(o_ref_in[0], w_ref[...], preferred_element_type=F32)


def _attn_out(x, o, mod3, row_of_b, w):
    b, l, d = x.shape
    tm = min(512, l)
    return pl.pallas_call(
        _attn_out_kernel, name="attn_out",
        grid=(b, l // tm),
        in_specs=[_tok_spec(tm, d), _tok_spec(tm, o.shape[2]), _mod_spec(d, 2, row_of_b), _full_spec(w.shape)],
        out_specs=_tok_spec(tm, d),
        out_shape=jax.ShapeDtypeStruct((b, l, d), F32),
        compiler_params=_params("parallel", "parallel"),
    )(x, o, mod3, w)


def _rope_tables(l):
    n = ATTN_HEAD_DIM // 4
    t = jnp.arange(l, dtype=jnp.int32)
    inv = ROPE_THETA ** (-jnp.arange(n, dtype=F32) / n)
    ang = jnp.concatenate([(t // GRID_W).astype(F32)[:, None] * inv,
                           (t % GRID_W).astype(F32)[:, None] * inv], axis=-1)
    cos, sin = jnp.cos(ang), jnp.sin(ang)
    cos_h = jnp.concatenate([cos, cos], axis=-1)
    sin_h = jnp.concatenate([-sin, sin], axis=-1)
    return jnp.concatenate([cos_h, cos_h], axis=-1), jnp.concatenate([sin_h, sin_h], axis=-1)


def _attn_layer(x_lat, x_ctx, mod3, nb, g_mix, w_qkv, g_q, g_k, w_out, need_ctx):
    l, lc = x_lat.shape[1], x_ctx.shape[1]
    w_qkv, w_out = w_qkv.astype(BF16), w_out.astype(BF16)
    lat_row, ctx_row = (lambda b: b), (lambda b: nb)
    cos, sin = _rope_tables(l)
    ql, kl, vl = _attn_proj(x_lat, mod3, lat_row, g_mix, w_qkv, g_q, g_k, cos, sin)
    qc, kc, vc = _attn_proj(x_ctx, mod3, ctx_row, g_mix, w_qkv, g_q, g_k,
                            jnp.ones((lc, 128), F32), jnp.zeros((lc, 128), F32))
    score_bound = ATTN_HEAD_DIM * Q_PRESCALE * jnp.max(jnp.abs(g_q)) * jnp.max(jnp.abs(g_k))
    unshifted = (score_bound <= UNSHIFTED_SCORE_LIMIT).astype(jnp.int32).reshape(1)
    o_lat = _flash(ql, jnp.concatenate([kc, kl], axis=2), jnp.concatenate([vc, vl], axis=3), unshifted)
    x_lat = _attn_out(x_lat, o_lat, mod3, lat_row, w_out)
    if need_ctx:
        x_ctx = _attn_out(x_ctx, _flash(qc, kc, vc, unshifted), mod3, ctx_row, w_out)
    return x_lat, x_ctx


def kernel(x, c, ctx, c_ctx, w_mod, b_mod, g_norm_mix, g_norm_mlp, w_mlp_in, w_mlp_out,
           gla_w_in, gla_w_g1, gla_w_g2, gla_b_g, gla_g_head, gla_w_out,
           conv_w_pw1, conv_b_pw1, conv_w_dw, conv_b_dw, conv_ln_g, conv_ln_b, conv_w_pw2, conv_b_pw2,
           attn_w_qkv, attn_g_q, attn_g_k, attn_w_out):
    nb, _, d = x.shape
    depth = w_mod.shape[0]
    mod = _modulation(c, c_ctx, w_mod, b_mod)
    x_lat, x_ctx = x, ctx
    for i in range(depth):
        last = i == depth - 1
        kind, j = i % 3, i // 3
        mod3 = mod[i].reshape(MOD_ROWS, 1, 6 * d)
        if kind == 0:
            x_lat, x_ctx = _gla_layer(x_lat, x_ctx, mod3, nb, g_norm_mix[i], gla_w_in[j], gla_w_g1[j],
                                      gla_w_g2[j], gla_b_g[j], gla_g_head[j], gla_w_out[j], not last)
        elif kind == 1:
            x_lat, x_ctx = _conv_layer(x_lat, x_ctx, mod3, nb, g_norm_mix[i], conv_w_pw1[j], conv_b_pw1[j],
                                       conv_w_dw[j], conv_b_dw[j], conv_ln_g[j], conv_ln_b[j],
                                       conv_w_pw2[j], conv_b_pw2[j], not last)
        else:
            x_lat, x_ctx = _attn_layer(x_lat, x_ctx, mod3, nb, g_norm_mix[i], attn_w_qkv[j], attn_g_q[j],
                                       attn_g_k[j], attn_w_out[j], not last)
        w_in, w_out = w_mlp_in[i].astype(BF16), w_mlp_out[i].astype(BF16)
        x_lat = _mlp(x_lat, mod3, lambda b: b, g_norm_mlp[i], w_in, w_out)
        if not last:
            x_ctx = _mlp(x_ctx, mod3, lambda b: nb, g_norm_mlp[i], w_in, w_out)
    return x_lat
```

```python
import functools

import jax
import jax.numpy as jnp
from jax import lax
from jax.experimental import pallas as pl
from jax.experimental.pallas import tpu as pltpu

F32 = jnp.float32
BF16 = jnp.bfloat16

EPS = 1e-6
GRID_W = 64
ROPE_THETA = 10000.0

GLA_HEADS = 4
GLA_TAU = 16.0
GLA_CHUNK = 64
CHUNK_SHIFT = GLA_CHUNK.bit_length() - 1
assert 1 << CHUNK_SHIFT == GLA_CHUNK

ATTN_HEAD_DIM = 64
ATTN_KV_HEADS = 4
LOG2E = 1.4426950408889634
Q_PRESCALE = ATTN_HEAD_DIM ** -0.5 * LOG2E
VT_ROWS = ATTN_HEAD_DIM + 16
UNSHIFTED_SCORE_LIMIT = 40.0

MOD_ROWS = 8
VMEM_LIMIT = 48 * 1024 * 1024

NT_DIMS = (((1,), (1,)), ((), ()))
TN_DIMS = (((0,), (0,)), ((), ()))


def _params(*sem):
    return pltpu.CompilerParams(dimension_semantics=sem, vmem_limit_bytes=VMEM_LIMIT)


def _sigmoid(x):
    return 1.0 / (1.0 + jnp.exp(-x))


def _rms_mod(x, g, sc, sh):
    ms = jnp.mean(x * x, axis=-1, keepdims=True)
    return (x * lax.rsqrt(ms + EPS) * g) * (1.0 + sc) + sh


def _mod_spec(d, col, row_of_b):
    return pl.BlockSpec((1, 1, d), lambda b, *_: (row_of_b(b), 0, col))


def _tok_spec(tm, width, col=0):
    return pl.BlockSpec((1, tm, width), lambda b, i, *_: (b, i, col))


def _full_spec(shape):
    zeros = (0,) * len(shape)
    return pl.BlockSpec(shape, lambda *_: zeros)


def _mod_kernel(ct_ref, w_ref, b_ref, o_ref, *, rows_used):
    ct = ct_ref[...]
    s = ct * _sigmoid(ct)
    w = w_ref[0]
    bias = b_ref[0]
    rows = []
    for r in range(MOD_ROWS):
        if r < rows_used:
            rows.append(jnp.sum(w * s[:, r:r + 1], axis=0, keepdims=True) + bias)
        else:
            rows.append(jnp.zeros_like(bias))
    o_ref[0] = jnp.concatenate(rows, axis=0)


def _modulation(c, c_ctx, w_mod, b_mod):
    depth, d, n = w_mod.shape
    b = c.shape[0]
    assert b + 1 <= MOD_ROWS
    cvec = jnp.zeros((MOD_ROWS, d), F32).at[:b].set(c).at[b].set(c_ctx)
    tn = 1024
    return pl.pallas_call(
        functools.partial(_mod_kernel, rows_used=b + 1), name="modulation",
        grid=(depth, n // tn),
        in_specs=[_full_spec((d, MOD_ROWS)),
                  pl.BlockSpec((1, d, tn), lambda i, j: (i, 0, j)),
                  pl.BlockSpec((1, 1, tn), lambda i, j: (i, 0, j))],
        out_specs=pl.BlockSpec((1, MOD_ROWS, tn), lambda i, j: (i, 0, j)),
        out_shape=jax.ShapeDtypeStruct((depth, MOD_ROWS, n), F32),
        compiler_params=_params("parallel", "parallel"),
    )(cvec.T, w_mod, b_mod.reshape(depth, 1, n))


def _mlp_kernel(x_ref, sh_ref, sc_ref, gt_ref, g_ref, win_ref, wout_ref, o_ref, h_scr, acc_scr):
    f = pl.program_id(2)

    @pl.when(f == 0)
    def _():
        h = _rms_mod(x_ref[0], g_ref[...], sc_ref[0], sh_ref[0])
        h_scr[...] = h.astype(BF16)
        acc_scr[...] = jnp.zeros_like(acc_scr)

    a = jnp.dot(h_scr[...], win_ref[...], preferred_element_type=F32)
    a = jnp.square(jnp.maximum(a, 0.0))
    acc_scr[...] += jnp.dot(a.astype(BF16), wout_ref[...], preferred_element_type=F32)

    @pl.when(f == pl.num_programs(2) - 1)
    def _():
        o_ref[0] = x_ref[0] + gt_ref[0] * acc_scr[...]


def _mlp(x, mod3, row_of_b, g, w_in, w_out):
    b, l, d = x.shape
    dff = w_in.shape[1]
    tm = min(1024, l)
    tf = 512
    return pl.pallas_call(
        _mlp_kernel, name="mlp",
        grid=(b, l // tm, dff // tf),
        in_specs=[_tok_spec(tm, d),
                  _mod_spec(d, 3, row_of_b), _mod_spec(d, 4, row_of_b), _mod_spec(d, 5, row_of_b),
                  _full_spec((1, d)),
                  pl.BlockSpec((d, tf), lambda bb, i, f: (0, f)),
                  pl.BlockSpec((tf, d), lambda bb, i, f: (f, 0))],
        out_specs=_tok_spec(tm, d),
        out_shape=jax.ShapeDtypeStruct((b, l, d), F32),
        scratch_shapes=[pltpu.VMEM((tm, d), BF16), pltpu.VMEM((tm, d), F32)],
        compiler_params=_params("parallel", "parallel", "arbitrary"),
    )(x, mod3, mod3, mod3, g.reshape(1, d), w_in, w_out)


def _split2(x):
    hi = x.astype(BF16)
    lo = (x - hi.astype(F32)).astype(BF16)
    return hi, lo


def _gla_proj_kernel(x_ref, sh_ref, sc_ref, g_ref, win_ref, wg1_ref, wg2_ref, bg_ref,
                     qbf_ref, kbf_ref, kdf_ref, decf_ref, qbb_ref, kbb_ref, kdb_ref, decb_ref,
                     v_ref, r_ref, *, kdim, vdim):
    tm = x_ref.shape[1]
    h = _rms_mod(x_ref[0], g_ref[...], sc_ref[0], sh_ref[0]).astype(BF16)
    t = jnp.dot(h, wg1_ref[...], preferred_element_type=F32)
    zg = jnp.dot(t.astype(BF16), wg2_ref[...], preferred_element_type=F32) + bg_ref[...]
    logg = (jnp.minimum(zg, 0.0) - jnp.log1p(jnp.exp(-jnp.abs(zg)))) * (1.0 / GLA_TAU)
    zqk = jnp.dot(h, win_ref[:, :2 * kdim], preferred_element_type=F32)
    q = zqk[:, :kdim] * ((kdim // GLA_HEADS) ** -0.5)
    k = zqk[:, kdim:]

    grp = min(tm, 256)
    row = lax.broadcasted_iota(jnp.int32, (grp, grp), 0)
    col = lax.broadcasted_iota(jnp.int32, (grp, grp), 1)
    same = (row >> CHUNK_SHIFT) == (col >> CHUNK_SHIFT)
    tri_f = (same & (col <= row)).astype(BF16)
    tri_b = (same & (col >= row)).astype(BF16)
    nch = tm // GLA_CHUNK
    gch = grp // GLA_CHUNK
    srow = lax.broadcasted_iota(jnp.int32, (8, tm), 0)
    scol = lax.broadcasted_iota(jnp.int32, (8, tm), 1)
    sel = (srow == (scol >> CHUNK_SHIFT)).astype(BF16)

    def msum(m, hi, lo):
        return (jnp.dot(m, hi, preferred_element_type=F32)
                + jnp.dot(m, lo, preferred_element_type=F32))

    outs = ((qbf_ref, kbf_ref, kdf_ref, decf_ref, tri_f), (qbb_ref, kbb_ref, kdb_ref, decb_ref, tri_b))
    sums = []
    for d, (_, _, _, _, tri) in enumerate(outs):
        hi, lo = _split2(logg[:, d * kdim:(d + 1) * kdim])
        tot_c = msum(sel, hi, lo)
        cums = [msum(tri, hi[gi * grp:(gi + 1) * grp], lo[gi * grp:(gi + 1) * grp])
                for gi in range(tm // grp)]
        sums.append((tot_c, cums))
    zvr = jnp.dot(h, win_ref[:, 2 * kdim:], preferred_element_type=F32)
    for (qb_ref, kb_ref, kd_ref, dec_ref, _), (tot_c, cums) in zip(outs, sums):
        dec_ref[0] = jnp.exp(tot_c[:nch])
        for gi, cum in enumerate(cums):
            rs = slice(gi * grp, (gi + 1) * grp)
            tot = jnp.concatenate(
                [jnp.broadcast_to(tot_c[c:c + 1], (GLA_CHUNK, kdim)) for c in range(gi * gch, (gi + 1) * gch)],
                axis=0)
            qb_ref[0, rs, :] = (q[rs] * jnp.exp(cum)).astype(BF16)
            kb_ref[0, rs, :] = (k[rs] * jnp.exp(-cum)).astype(BF16)
            kd_ref[0, rs, :] = (k[rs] * jnp.exp(tot - cum)).astype(BF16)
    v_ref[0] = zvr[:, :vdim].astype(BF16)
    r_ref[0] = zvr[:, vdim:].astype(BF16)


def _gla_proj(x, mod3, row_of_b, g, w_in, wg1, wg2, bg, kdim, vdim):
    b, l, d = x.shape
    tm = min(512, l)
    nc = tm // GLA_CHUNK
    tok_bf = lambda w: jax.ShapeDtypeStruct((b, l, w), BF16)
    dec = jax.ShapeDtypeStruct((b, l // GLA_CHUNK, kdim), F32)
    dec_spec = pl.BlockSpec((1, nc, kdim), lambda bb, i: (bb, i, 0))
    dir_shapes = [tok_bf(kdim), tok_bf(kdim), tok_bf(kdim), dec]
    dir_specs = [_tok_spec(tm, kdim)] * 3 + [dec_spec]
    return pl.pallas_call(
        functools.partial(_gla_proj_kernel, kdim=kdim, vdim=vdim), name="gla_proj",
        grid=(b, l // tm),
        in_specs=[_tok_spec(tm, d), _mod_spec(d, 0, row_of_b), _mod_spec(d, 1, row_of_b),
                  _full_spec((1, d)), _full_spec(w_in.shape), _full_spec(wg1.shape),
                  _full_spec(wg2.shape), _full_spec(bg.shape)],
        out_specs=dir_specs + dir_specs + [_tok_spec(tm, vdim), _tok_spec(tm, vdim)],
        out_shape=dir_shapes + dir_shapes + [tok_bf(vdim), tok_bf(vdim)],
        compiler_params=_params("parallel", "parallel"),
    )(x, mod3, mod3, g.reshape(1, d), w_in, wg1, wg2, bg)


def _gla_scan_kernel(qf_ref, kf_ref, df_ref, cf_ref, vf_ref, qr_ref, kr_ref, dr_ref, cr_ref, vr_ref,
                     s0f_ref, s0r_ref, of_ref, or_ref, sff_ref, sfr_ref, s_scr, *, nchunk, dk, dv):
    j = pl.program_id(1)

    @pl.when(j == 0)
    def _():
        s_scr[0] = s0f_ref[0]
        s_scr[1] = s0r_ref[0]

    row = lax.broadcasted_iota(jnp.int32, (GLA_CHUNK, GLA_CHUNK), 0)
    col = lax.broadcasted_iota(jnp.int32, (GLA_CHUNK, GLA_CHUNK), 1)
    dirs = ((qf_ref, kf_ref, df_ref, cf_ref, vf_ref, of_ref, col <= row),
            (qr_ref, kr_ref, dr_ref, cr_ref, vr_ref, or_ref, col >= row))
    chains = [(d, h) for d in range(2) for h in range(GLA_HEADS)]
    for step in range(nchunk):
        local = []
        for d, h in chains:
            qb_ref, kb_ref, kd_ref, _, v_ref, _, mask = dirs[d]
            c = step if d == 0 else nchunk - 1 - step
            rs = slice(c * GLA_CHUNK, (c + 1) * GLA_CHUNK)
            ks = slice(h * dk, (h + 1) * dk)
            v = v_ref[0, rs, h * dv:(h + 1) * dv]
            a = lax.dot_general(qb_ref[0, rs, ks], kb_ref[0, rs, ks], NT_DIMS, preferred_element_type=F32)
            u = lax.dot_general(v, kd_ref[0, rs, ks], TN_DIMS, preferred_element_type=F32)
            local.append((jnp.where(mask, a, 0.0).astype(BF16), u))
        for (d, h), (a, u) in zip(chains, local):
            qb_ref, _, _, dec_ref, v_ref, o_ref, _ = dirs[d]
            c = step if d == 0 else nchunk - 1 - step
            rs = slice(c * GLA_CHUNK, (c + 1) * GLA_CHUNK)
            ks = slice(h * dk, (h + 1) * dk)
            vs = slice(h * dv, (h + 1) * dv)
            s_t = s_scr[d, h]
            o = (lax.dot_general(qb_ref[0, rs, ks], s_t.astype(BF16), NT_DIMS, preferred_element_type=F32)
                 + jnp.dot(a, v_ref[0, rs, vs], preferred_element_type=F32))
            o_ref[0, rs, vs] = o.astype(BF16)
            s_scr[d, h] = dec_ref[0, c:c + 1, ks] * s_t + u

    @pl.when(j == pl.num_programs(1) - 1)
    def _():
        sff_ref[0] = s_scr[0]
        sfr_ref[0] = s_scr[1]


def _gla_scan(p, s0f, s0r):
    qf, kf, df, cf, qr, kr, dr, cr, v = p[:9]
    b, l, kdim = qf.shape
    vdim = v.shape[2]
    dk, dv = kdim // GLA_HEADS, vdim // GLA_HEADS
    tc = min(512, l)
    n = l // tc
    nchunk = tc // GLA_CHUNK
    fwd = lambda w: pl.BlockSpec((1, tc, w), lambda bb, j: (bb, j, 0))
    rev = lambda w: pl.BlockSpec((1, tc, w), lambda bb, j: (bb, n - 1 - j, 0))
    dec_f = pl.BlockSpec((1, nchunk, kdim), lambda bb, j: (bb, j, 0))
    dec_r = pl.BlockSpec((1, nchunk, kdim), lambda bb, j: (bb, n - 1 - j, 0))
    sspec = pl.BlockSpec((1, GLA_HEADS, dv, dk), lambda bb, j: (bb, 0, 0, 0))
    o_shape = jax.ShapeDtypeStruct((b, l, vdim), BF16)
    s_shape = jax.ShapeDtypeStruct((b, GLA_HEADS, dv, dk), F32)
    return pl.pallas_call(
        functools.partial(_gla_scan_kernel, nchunk=nchunk, dk=dk, dv=dv), name="gla_scan",
        grid=(b, n),
        in_specs=[fwd(kdim), fwd(kdim), fwd(kdim), dec_f, fwd(vdim),
                  rev(kdim), rev(kdim), rev(kdim), dec_r, rev(vdim), sspec, sspec],
        out_specs=[fwd(vdim), rev(vdim), sspec, sspec],
        out_shape=[o_shape, o_shape, s_shape, s_shape],
        scratch_shapes=[pltpu.VMEM((2, GLA_HEADS, dv, dk), F32)],
        compiler_params=_params("parallel", "arbitrary"),
    )(qf, kf, df, cf, v, qr, kr, dr, cr, v, s0f, s0r)


def _gla_finish_kernel(x_ref, of_ref, ob_ref, r_ref, gt_ref, gh_ref, wout_ref, o_ref):
    o = of_ref[0].astype(F32) + ob_ref[0].astype(F32)
    dv = gh_ref.shape[1]
    parts = []
    for hd in range(o.shape[1] // dv):
        oh = o[:, hd * dv:(hd + 1) * dv]
        ms = jnp.mean(oh * oh, axis=-1, keepdims=True)
        parts.append(oh * lax.rsqrt(ms + EPS) * gh_ref[...])
    on = jnp.concatenate(parts, axis=-1)
    r = r_ref[0].astype(F32)
    y = on * (r * _sigmoid(r))
    o_ref[0] = x_ref[0] + gt_ref[0] * jnp.dot(y.astype(BF16), wout_ref[...], preferred_element_type=F32)


def _gla_finish(x, o_f, o_b, r, mod3, row_of_b, g_head, w_out):
    b, l, d = x.shape
    vdim = r.shape[2]
    tm = min(512, l)
    return pl.pallas_call(
        _gla_finish_kernel, name="gla_finish",
        grid=(b, l // tm),
        in_specs=[_tok_spec(tm, d), _tok_spec(tm, vdim), _tok_spec(tm, vdim), _tok_spec(tm, vdim),
                  _mod_spec(d, 2, row_of_b), _full_spec((1, g_head.shape[0])), _full_spec(w_out.shape)],
        out_specs=_tok_spec(tm, d),
        out_shape=jax.ShapeDtypeStruct((b, l, d), F32),
        compiler_params=_params("parallel", "parallel"),
    )(x, o_f, o_b, r, mod3, g_head.reshape(1, -1), w_out)


def _gla_weights(w_in, w_g1, w_g2, b_g):
    rank, kdim = w_g2.shape[1], w_g2.shape[2]
    wg1 = jnp.concatenate([w_g1[0], w_g1[1]], axis=1).astype(BF16)
    wg2 = jnp.zeros((2 * rank, 2 * kdim), F32)
    wg2 = wg2.at[:rank, :kdim].set(w_g2[0]).at[rank:, kdim:].set(w_g2[1]).astype(BF16)
    bg = jnp.concatenate([b_g[0], b_g[1]])[None, :]
    return w_in.astype(BF16), wg1, wg2, bg


def _gla_layer(x_lat, x_ctx, mod3, nb, g_mix, w_in, w_g1, w_g2, b_g, g_head, w_out, need_ctx):
    kdim = w_g2.shape[2]
    vdim = w_out.shape[0]
    w_in, wg1, wg2, bg = _gla_weights(w_in, w_g1, w_g2, b_g)
    w_out = w_out.astype(BF16)
    lat_row, ctx_row = (lambda b: b), (lambda b: nb)
    pc = _gla_proj(x_ctx, mod3, ctx_row, g_mix, w_in, wg1, wg2, bg, kdim, vdim)
    pq = _gla_proj(x_lat, mod3, lat_row, g_mix, w_in, wg1, wg2, bg, kdim, vdim)
    s0 = jnp.zeros((nb, GLA_HEADS, vdim // GLA_HEADS, kdim // GLA_HEADS), F32)
    oc_f, oc_b, s_f, s_b = _gla_scan(pc, s0, s0)
    ol_f, ol_b, _, _ = _gla_scan(pq, s_f, s_b)
    x_lat = _gla_finish(x_lat, ol_f, ol_b, pq[9], mod3, lat_row, g_head, w_out)
    if need_ctx:
        x_ctx = _gla_finish(x_ctx, oc_f, oc_b, pc[9], mod3, ctx_row, g_head, w_out)
    return x_lat, x_ctx


def _conv_pw1_kernel(x_ref, sh_ref, sc_ref, g_ref, w_ref, b_ref, u_ref):
    d = x_ref.shape[2]
    h = _rms_mod(x_ref[0], g_ref[...], sc_ref[0], sh_ref[0]).astype(BF16)
    z = jnp.dot(h, w_ref[...], preferred_element_type=F32) + b_ref[...]
    u_ref[0] = z[:, :d] * _sigmoid(z[:, d:])


def _conv_pw1(x, mod3, row_of_b, g, w, bias):
    b, l, d = x.shape
    tm = min(512, l)
    return pl.pallas_call(
        _conv_pw1_kernel, name="conv_pw1",
        grid=(b, l // tm),
        in_specs=[_tok_spec(tm, d), _mod_spec(d, 0, row_of_b), _mod_spec(d, 1, row_of_b),
                  _full_spec((1, d)), _full_spec(w.shape), _full_spec((1, 2 * d))],
        out_specs=_tok_spec(tm, d),
        out_shape=jax.ShapeDtypeStruct((b, l, d), F32),
        compiler_params=_params("parallel", "parallel"),
    )(x, mod3, mod3, g.reshape(1, d), w, bias.reshape(1, 2 * d))


CONV_HALO = 16
CONV_RB = 64


def _conv_dw_kernel(x_ref, up_ref, uc_ref, un_ref, gt_ref, wdw_ref, bdw_ref, lng_ref, lnb_ref,
                    w2_ref, b2_ref, o_ref, ubuf, sbuf, cbuf, *, width):
    i = pl.program_id(1)
    tm, d = uc_ref.shape[1], uc_ref.shape[2]
    pad = width // 2
    ubuf[0:CONV_HALO, :] = jnp.where(i > 0, up_ref[0], 0.0)
    ubuf[CONV_HALO:CONV_HALO + tm, :] = uc_ref[0]
    ubuf[CONV_HALO + tm:, :] = jnp.where(i < pl.num_programs(1) - 1, un_ref[0], 0.0)
    srows = sbuf.shape[1]
    for s in range(1, 8):
        sbuf[s - 1] = ubuf[s:s + srows, :]
    off = CONV_HALO - pad
    for cb in range(d // 128):
        cs = slice(cb * 128, (cb + 1) * 128)
        w = wdw_ref[:, cs]
        for rb in range(tm // CONV_RB):
            acc = jnp.zeros((CONV_RB, 128), F32)
            for k in range(width):
                start = rb * CONV_RB + (off + k) // 8 * 8
                s = (off + k) % 8
                rows = ubuf[start:start + CONV_RB, cs] if s == 0 else sbuf[s - 1, start:start + CONV_RB, cs]
                acc = acc + w[k:k + 1, :] * rows
            cbuf[rb * CONV_RB:(rb + 1) * CONV_RB, cs] = acc
    u = cbuf[...] + bdw_ref[...]
    mu = jnp.mean(u, axis=-1, keepdims=True)
    uc = u - mu
    y = uc * lax.rsqrt(jnp.mean(uc * uc, axis=-1, keepdims=True) + EPS) * lng_ref[...] + lnb_ref[...]
    y = y * _sigmoid(y)
    y = jnp.dot(y.astype(BF16), w2_ref[...], preferred_element_type=F32) + b2_ref[...]
    o_ref[0] = x_ref[0] + gt_ref[0] * y


def _conv_dw(x, u, mod3, row_of_b, w_dw, b_dw, ln_g, ln_b, w2, b2):
    b, l, d = x.shape
    width = w_dw.shape[0]
    assert width // 2 <= CONV_HALO
    tm = min(256, l)
    nh = tm // CONV_HALO
    last = l // CONV_HALO - 1
    wpad = jnp.zeros((-(-width // 8) * 8, d), F32).at[:width].set(w_dw)
    vec = lambda a: a.reshape(1, d)
    return pl.pallas_call(
        functools.partial(_conv_dw_kernel, width=width), name="conv_dw",
        grid=(b, l // tm),
        in_specs=[_tok_spec(tm, d),
                  pl.BlockSpec((1, CONV_HALO, d), lambda bb, i: (bb, jnp.maximum(i * nh - 1, 0), 0)),
                  _tok_spec(tm, d),
                  pl.BlockSpec((1, CONV_HALO, d), lambda bb, i: (bb, jnp.minimum((i + 1) * nh, last), 0)),
                  _mod_spec(d, 2, row_of_b), _full_spec(wpad.shape), _full_spec((1, d)),
                  _full_spec((1, d)), _full_spec((1, d)), _full_spec(w2.shape), _full_spec((1, d))],
        out_specs=_tok_spec(tm, d),
        out_shape=jax.ShapeDtypeStruct((b, l, d), F32),
        scratch_shapes=[pltpu.VMEM((tm + 2 * CONV_HALO, d), F32),
                        pltpu.VMEM((7, tm + 2 * CONV_HALO - 8, d), F32),
                        pltpu.VMEM((tm, d), F32)],
        compiler_params=_params("parallel", "parallel"),
    )(x, u, u, u, mod3, wpad, vec(b_dw), vec(ln_g), vec(ln_b), w2, vec(b2))


def _conv_layer(x_lat, x_ctx, mod3, nb, g_mix, w_pw1, b_pw1, w_dw, b_dw, ln_g, ln_b, w_pw2, b_pw2, need_ctx):
    w1, w2 = w_pw1.astype(BF16), w_pw2.astype(BF16)
    streams = [(x_lat, lambda b: b)] + ([(x_ctx, lambda b: nb)] if need_ctx else [])
    outs = []
    for xs, row in streams:
        u = _conv_pw1(xs, mod3, row, g_mix, w1, b_pw1)
        outs.append(_conv_dw(xs, u, mod3, row, w_dw, b_dw, ln_g, ln_b, w2, b_pw2))
    return outs[0], (outs[1] if need_ctx else x_ctx)


def _attn_proj_kernel(x_ref, sh_ref, sc_ref, g_ref, w_ref, gq_ref, gk_ref, cos_ref, sin_ref,
                      q_ref, k_ref, vt_ref, *, qdim, kvdim):
    hd = ATTN_HEAD_DIM
    h = _rms_mod(x_ref[0], g_ref[...], sc_ref[0], sh_ref[0]).astype(BF16)
    z = jnp.dot(h, w_ref[...], preferred_element_type=F32)
    row = lax.broadcasted_iota(jnp.int32, (128, 128), 0)
    col = lax.broadcasted_iota(jnp.int32, (128, 128), 1)
    shift = hd.bit_length() - 1
    head_ones = ((row >> shift) == (col >> shift)).astype(BF16)
    lane = lax.broadcasted_iota(jnp.int32, (1, 128), 1)
    upper = (lane & (hd - 1)) >= (hd // 2)
    cos = cos_ref[...]
    sin = sin_ref[...]

    def norm_rope(blk, gain):
        ssq = jnp.dot((blk * blk).astype(BF16), head_ones, preferred_element_type=F32)
        n = blk * lax.rsqrt(ssq * (1.0 / hd) + EPS) * gain
        partner = jnp.where(upper, pltpu.roll(n, hd // 2, 1), pltpu.roll(n, 128 - hd // 2, 1))
        return n * cos + partner * sin

    tm = x_ref.shape[1]
    ones_rows = (lax.broadcasted_iota(jnp.int32, (VT_ROWS - hd, tm), 0) == 0).astype(BF16)
    for j in range(qdim // 128):
        blk = norm_rope(z[:, j * 128:(j + 1) * 128], gq_ref[...]) * Q_PRESCALE
        q_ref[0, 2 * j] = blk[:, :hd].astype(BF16)
        q_ref[0, 2 * j + 1] = blk[:, hd:].astype(BF16)
    for j in range(kvdim // 128):
        blk = norm_rope(z[:, qdim + j * 128:qdim + (j + 1) * 128], gk_ref[...])
        k_ref[0, 2 * j] = blk[:, :hd].astype(BF16)
        k_ref[0, 2 * j + 1] = blk[:, hd:].astype(BF16)
        vbt = z[:, qdim + kvdim + j * 128:qdim + kvdim + (j + 1) * 128].T
        for half in range(2):
            vt_ref[0, 2 * j + half, 0:hd, :] = vbt[half * hd:(half + 1) * hd].astype(BF16)
            vt_ref[0, 2 * j + half, hd:VT_ROWS, :] = ones_rows


def _attn_proj(x, mod3, row_of_b, g, w, g_q, g_k, cos, sin):
    b, l, d = x.shape
    hd = ATTN_HEAD_DIM
    kvdim = ATTN_KV_HEADS * hd
    qdim = w.shape[1] - 2 * kvdim
    tm = min(512, l)
    tile2 = lambda a: jnp.concatenate([a, a]).reshape(1, 2 * hd)
    head_spec = lambda nh: pl.BlockSpec((1, nh, tm, hd), lambda bb, i: (bb, 0, i, 0))
    head_shape = lambda nh: jax.ShapeDtypeStruct((b, nh, l, hd), BF16)
    return pl.pallas_call(
        functools.partial(_attn_proj_kernel, qdim=qdim, kvdim=kvdim), name="attn_proj",
        grid=(b, l // tm),
        in_specs=[_tok_spec(tm, d), _mod_spec(d, 0, row_of_b), _mod_spec(d, 1, row_of_b),
                  _full_spec((1, d)), _full_spec(w.shape), _full_spec((1, 2 * hd)), _full_spec((1, 2 * hd)),
                  pl.BlockSpec((tm, 128), lambda bb, i: (i, 0)), pl.BlockSpec((tm, 128), lambda bb, i: (i, 0))],
        out_specs=[head_spec(qdim // hd), head_spec(ATTN_KV_HEADS),
                   pl.BlockSpec((1, ATTN_KV_HEADS, VT_ROWS, tm), lambda bb, i: (bb, 0, 0, i))],
        out_shape=[head_shape(qdim // hd), head_shape(ATTN_KV_HEADS),
                   jax.ShapeDtypeStruct((b, ATTN_KV_HEADS, VT_ROWS, l), BF16)],
        compiler_params=_params("parallel", "parallel"),
    )(x, mod3, mod3, g.reshape(1, d), w, tile2(g_q), tile2(g_k), cos, sin)


def _flash_kernel(unshifted_ref, q_ref, k_ref, vt_ref, o_ref):
    grp, tq, hd = q_ref.shape[1], q_ref.shape[2], q_ref.shape[3]
    rows = grp * tq
    q = q_ref[0].reshape(rows, hd)
    nk, tk = vt_ref.shape[2], vt_ref.shape[4]

    def scores_t(c):
        start = pl.multiple_of(c * tk, tk)
        return lax.dot_general(k_ref[0, 0, pl.ds(start, tk), :], q, NT_DIMS, preferred_element_type=F32)

    def finish(acc):
        o_t = acc[:hd] / acc[hd:hd + 1]
        o_ref[0] = jnp.concatenate([o_t[:, g * tq:(g + 1) * tq].T for g in range(grp)],
                                   axis=-1).astype(BF16)

    @pl.when(unshifted_ref[0] == 1)
    def _():
        def body(c, acc):
            p_t = jnp.exp2(scores_t(c)).astype(BF16)
            return acc + jnp.dot(vt_ref[0, 0, c], p_t, preferred_element_type=F32)

        finish(lax.fori_loop(0, nk, body, jnp.zeros((VT_ROWS, rows), F32), unroll=True))

    @pl.when(unshifted_ref[0] != 1)
    def _():
        def body(c, carry):
            m, acc = carry
            s_t = scores_t(c)
            m_new = jnp.maximum(m, jnp.max(s_t, axis=0, keepdims=True))
            p_t = jnp.exp2(s_t - m_new).astype(BF16)
            acc = jnp.exp2(m - m_new) * acc + jnp.dot(vt_ref[0, 0, c], p_t, preferred_element_type=F32)
            return m_new, acc

        init = (jnp.full((1, rows), -jnp.inf, F32), jnp.zeros((VT_ROWS, rows), F32))
        finish(lax.fori_loop(0, nk, body, init)[1])


def _flash(q, k, vt, unshifted):
    b, nq, l, hd = q.shape
    nkv, s = k.shape[1], k.shape[2]
    grp = nq // nkv
    tq = min(128, l)
    tk = 1280 if s % 1280 == 0 else 256
    assert s % tk == 0
    vt = vt.reshape(b, nkv, VT_ROWS, s // tk, tk).transpose(0, 1, 3, 2, 4)
    return pl.pallas_call(
        _flash_kernel, name="flash",
        grid_spec=pltpu.PrefetchScalarGridSpec(
            num_scalar_prefetch=1,
            grid=(b, nkv, l // tq),
            in_specs=[pl.BlockSpec((1, grp, tq, hd), lambda bb, h, i, *_: (bb, h, i, 0)),
                      pl.BlockSpec((1, 1, s, hd), lambda bb, h, i, *_: (bb, h, 0, 0)),
                      pl.BlockSpec((1, 1, s // tk, VT_ROWS, tk), lambda bb, h, i, *_: (bb, h, 0, 0, 0))],
            out_specs=pl.BlockSpec((1, tq, grp * hd), lambda bb, h, i, *_: (bb, i, h))),
        out_shape=jax.ShapeDtypeStruct((b, l, nq * hd), BF16),
        compiler_params=_params("parallel", "parallel", "parallel"),
    )(unshifted, q, k, vt)


def _attn_out_kernel(x_ref, o_ref_in, gt_ref, w_ref, o_ref):
    o_ref[0] = x_ref[0] + gt_ref[0] * jnp.dot(o_ref_in[0], w_ref[...], preferred_element_type=F32)


def _attn_out(x, o, mod3, row_of_b, w):
    b, l, d = x.shape
    tm = min(512, l)
    return pl.pallas_call(
        _attn_out_kernel, name="attn_out",
        grid=(b, l // tm),
        in_specs=[_tok_spec(tm, d), _tok_spec(tm, o.shape[2]), _mod_spec(d, 2, row_of_b), _full_spec(w.shape)],
        out_specs=_tok_spec(tm, d),
        out_shape=jax.ShapeDtypeStruct((b, l, d), F32),
        compiler_params=_params("parallel", "parallel"),
    )(x, o, mod3, w)


def _rope_tables(l):
    n = ATTN_HEAD_DIM // 4
    t = jnp.arange(l, dtype=jnp.int32)
    inv = ROPE_THETA ** (-jnp.arange(n, dtype=F32) / n)
    ang = jnp.concatenate([(t // GRID_W).astype(F32)[:, None] * inv,
                           (t % GRID_W).astype(F32)[:, None] * inv], axis=-1)
    cos, sin = jnp.cos(ang), jnp.sin(ang)
    cos_h = jnp.concatenate([cos, cos], axis=-1)
    sin_h = jnp.concatenate([-sin, sin], axis=-1)
    return jnp.concatenate([cos_h, cos_h], axis=-1), jnp.concatenate([sin_h, sin_h], axis=-1)


def _attn_layer(x_lat, x_ctx, mod3, nb, g_mix, w_qkv, g_q, g_k, w_out, need_ctx):
    l, lc = x_lat.shape[1], x_ctx.shape[1]
    w_qkv, w_out = w_qkv.astype(BF16), w_out.astype(BF16)
    lat_row, ctx_row = (lambda b: b), (lambda b: nb)
    cos, sin = _rope_tables(l)
    ql, kl, vl = _attn_proj(x_lat, mod3, lat_row, g_mix, w_qkv, g_q, g_k, cos, sin)
    qc, kc, vc = _attn_proj(x_ctx, mod3, ctx_row, g_mix, w_qkv, g_q, g_k,
                            jnp.ones((lc, 128), F32), jnp.zeros((lc, 128), F32))
    score_bound = ATTN_HEAD_DIM * Q_PRESCALE * jnp.max(jnp.abs(g_q)) * jnp.max(jnp.abs(g_k))
    unshifted = (score_bound <= UNSHIFTED_SCORE_LIMIT).astype(jnp.int32).reshape(1)
    o_lat = _flash(ql, jnp.concatenate([kc, kl], axis=2), jnp.concatenate([vc, vl], axis=3), unshifted)
    x_lat = _attn_out(x_lat, o_lat, mod3, lat_row, w_out)
    if need_ctx:
        x_ctx = _attn_out(x_ctx, _flash(qc, kc, vc, unshifted), mod3, ctx_row, w_out)
    return x_lat, x_ctx


def kernel(x, c, ctx, c_ctx, w_mod, b_mod, g_norm_mix, g_norm_mlp, w_mlp_in, w_mlp_out,
           gla_w_in, gla_w_g1, gla_w_g2, gla_b_g, gla_g_head, gla_w_out,
           conv_w_pw1, conv_b_pw1, conv_w_dw, conv_b_dw, conv_ln_g, conv_ln_b, conv_w_pw2, conv_b_pw2,
           attn_w_qkv, attn_g_q, attn_g_k, attn_w_out):
    nb, _, d = x.shape
    depth = w_mod.shape[0]
    mod = _modulation(c, c_ctx, w_mod, b_mod)
    x_lat, x_ctx = x, ctx
    for i in range(depth):
        last = i == depth - 1
        kind, j = i % 3, i // 3
        mod3 = mod[i].reshape(MOD_ROWS, 1, 6 * d)
        if kind == 0:
            x_lat, x_ctx = _gla_layer(x_lat, x_ctx, mod3, nb, g_norm_mix[i], gla_w_in[j], gla_w_g1[j],
                                      gla_w_g2[j], gla_b_g[j], gla_g_head[j], gla_w_out[j], not last)
        elif kind == 1:
            x_lat, x_ctx = _conv_layer(x_lat, x_ctx, mod3, nb, g_norm_mix[i], conv_w_pw1[j], conv_b_pw1[j],
                                       conv_w_dw[j], conv_b_dw[j], conv_ln_g[j], conv_ln_b[j],
                                       conv_w_pw2[j], conv_b_pw2[j], not last)
        else:
            x_lat, x_ctx = _attn_layer(x_lat, x_ctx, mod3, nb, g_norm_mix[i], attn_w_qkv[j], attn_g_q[j],
                                       attn_g_k[j], attn_w_out[j], not last)
        w_in, w_out = w_mlp_in[i].astype(BF16), w_mlp_out[i].astype(BF16)
        x_lat = _mlp(x_lat, mod3, lambda b: b, g_norm_mlp[i], w_in, w_out)
        if not last:
            x_ctx = _mlp(x_ctx, mod3, lambda b: nb, g_norm_mlp[i], w_in, w_out)
    return x_lat
```

```python
import functools

import jax
import jax.numpy as jnp
from jax import lax
from jax.experimental import pallas as pl
from jax.experimental.pallas import tpu as pltpu

F32 = jnp.float32
BF16 = jnp.bfloat16

EPS = 1e-6
GRID_W = 64
ROPE_THETA = 10000.0

GLA_HEADS = 4
GLA_TAU = 16.0
GLA_CHUNK = 64
CHUNK_SHIFT = GLA_CHUNK.bit_length() - 1
assert 1 << CHUNK_SHIFT == GLA_CHUNK

ATTN_HEAD_DIM = 64
ATTN_KV_HEADS = 4
LOG2E = 1.4426950408889634
Q_PRESCALE = ATTN_HEAD_DIM ** -0.5 * LOG2E
VT_ROWS = ATTN_HEAD_DIM + 16
UNSHIFTED_SCORE_LIMIT = 40.0

MOD_ROWS = 8
VMEM_LIMIT = 48 * 1024 * 1024

NT_DIMS = (((1,), (1,)), ((), ()))
TN_DIMS = (((0,), (0,)), ((), ()))


def _params(*sem):
    return pltpu.CompilerParams(dimension_semantics=sem, vmem_limit_bytes=VMEM_LIMIT)


def _sigmoid(x):
    return 1.0 / (1.0 + jnp.exp(-x))


def _rms_mod(x, g, sc, sh):
    ms = jnp.mean(x * x, axis=-1, keepdims=True)
    return (x * lax.rsqrt(ms + EPS) * g) * (1.0 + sc) + sh


def _mod_spec(d, col, row_of_b):
    return pl.BlockSpec((1, 1, d), lambda b, *_: (row_of_b(b), 0, col))


def _tok_spec(tm, width, col=0):
    return pl.BlockSpec((1, tm, width), lambda b, i, *_: (b, i, col))


def _full_spec(shape):
    zeros = (0,) * len(shape)
    return pl.BlockSpec(shape, lambda *_: zeros)


def _mod_kernel(ct_ref, w_ref, b_ref, o_ref, *, rows_used):
    ct = ct_ref[...]
    s = ct * _sigmoid(ct)
    w = w_ref[0]
    bias = b_ref[0]
    rows = []
    for r in range(MOD_ROWS):
        if r < rows_used:
            rows.append(jnp.sum(w * s[:, r:r + 1], axis=0, keepdims=True) + bias)
        else:
            rows.append(jnp.zeros_like(bias))
    o_ref[0] = jnp.concatenate(rows, axis=0)


def _modulation(c, c_ctx, w_mod, b_mod):
    depth, d, n = w_mod.shape
    b = c.shape[0]
    assert b + 1 <= MOD_ROWS
    cvec = jnp.zeros((MOD_ROWS, d), F32).at[:b].set(c).at[b].set(c_ctx)
    tn = 1024
    return pl.pallas_call(
        functools.partial(_mod_kernel, rows_used=b + 1), name="modulation",
        grid=(depth, n // tn),
        in_specs=[_full_spec((d, MOD_ROWS)),
                  pl.BlockSpec((1, d, tn), lambda i, j: (i, 0, j)),
                  pl.BlockSpec((1, 1, tn), lambda i, j: (i, 0, j))],
        out_specs=pl.BlockSpec((1, MOD_ROWS, tn), lambda i, j: (i, 0, j)),
        out_shape=jax.ShapeDtypeStruct((depth, MOD_ROWS, n), F32),
        compiler_params=_params("parallel", "parallel"),
    )(cvec.T, w_mod, b_mod.reshape(depth, 1, n))


def _mlp_kernel(x_ref, sh_ref, sc_ref, gt_ref, g_ref, win_ref, wout_ref, o_ref, h_scr, acc_scr):
    f = pl.program_id(2)

    @pl.when(f == 0)
    def _():
        h = _rms_mod(x_ref[0], g_ref[...], sc_ref[0], sh_ref[0])
        h_scr[...] = h.astype(BF16)
        acc_scr[...] = jnp.zeros_like(acc_scr)

    a = jnp.dot(h_scr[...], win_ref[...], preferred_element_type=F32)
    a = jnp.square(jnp.maximum(a, 0.0))
    acc_scr[...] += jnp.dot(a.astype(BF16), wout_ref[...], preferred_element_type=F32)

    @pl.when(f == pl.num_programs(2) - 1)
    def _():
        o_ref[0] = x_ref[0] + gt_ref[0] * acc_scr[...]


def _mlp(x, mod3, row_of_b, g, w_in, w_out):
    b, l, d = x.shape
    dff = w_in.shape[1]
    tm = min(1024, l)
    tf = 1024
    return pl.pallas_call(
        _mlp_kernel, name="mlp",
        grid=(b, l // tm, dff // tf),
        in_specs=[_tok_spec(tm, d),
                  _mod_spec(d, 3, row_of_b), _mod_spec(d, 4, row_of_b), _mod_spec(d, 5, row_of_b),
                  _full_spec((1, d)),
                  pl.BlockSpec((d, tf), lambda bb, i, f: (0, f)),
                  pl.BlockSpec((tf, d), lambda bb, i, f: (f, 0))],
        out_specs=_tok_spec(tm, d),
        out_shape=jax.ShapeDtypeStruct((b, l, d), F32),
        scratch_shapes=[pltpu.VMEM((tm, d), BF16), pltpu.VMEM((tm, d), F32)],
        compiler_params=_params("parallel", "parallel", "arbitrary"),
    )(x, mod3, mod3, mod3, g.reshape(1, d), w_in, w_out)


def _split2(x):
    hi = x.astype(BF16)
    lo = (x - hi.astype(F32)).astype(BF16)
    return hi, lo


def _gla_proj_kernel(x_ref, sh_ref, sc_ref, g_ref, win_ref, wg1_ref, wg2_ref, bg_ref,
                     qbf_ref, kbf_ref, kdf_ref, decf_ref, qbb_ref, kbb_ref, kdb_ref, decb_ref,
                     v_ref, r_ref, *, kdim, vdim):
    tm = x_ref.shape[1]
    h = _rms_mod(x_ref[0], g_ref[...], sc_ref[0], sh_ref[0]).astype(BF16)
    t = jnp.dot(h, wg1_ref[...], preferred_element_type=F32)
    zg = jnp.dot(t.astype(BF16), wg2_ref[...], preferred_element_type=F32) + bg_ref[...]
    logg = (jnp.minimum(zg, 0.0) - jnp.log1p(jnp.exp(-jnp.abs(zg)))) * (1.0 / GLA_TAU)
    zqk = jnp.dot(h, win_ref[:, :2 * kdim], preferred_element_type=F32)
    q = zqk[:, :kdim] * ((kdim // GLA_HEADS) ** -0.5)
    k = zqk[:, kdim:]

    grp = min(tm, 256)
    row = lax.broadcasted_iota(jnp.int32, (grp, grp), 0)
    col = lax.broadcasted_iota(jnp.int32, (grp, grp), 1)
    same = (row >> CHUNK_SHIFT) == (col >> CHUNK_SHIFT)
    tri_f = (same & (col <= row)).astype(BF16)
    tri_b = (same & (col >= row)).astype(BF16)
    nch = tm // GLA_CHUNK
    gch = grp // GLA_CHUNK
    srow = lax.broadcasted_iota(jnp.int32, (8, tm), 0)
    scol = lax.broadcasted_iota(jnp.int32, (8, tm), 1)
    sel = (srow == (scol >> CHUNK_SHIFT)).astype(BF16)

    def msum(m, hi, lo):
        return (jnp.dot(m, hi, preferred_element_type=F32)
                + jnp.dot(m, lo, preferred_element_type=F32))

    outs = ((qbf_ref, kbf_ref, kdf_ref, decf_ref, tri_f), (qbb_ref, kbb_ref, kdb_ref, decb_ref, tri_b))
    sums = []
    for d, (_, _, _, _, tri) in enumerate(outs):
        hi, lo = _split2(logg[:, d * kdim:(d + 1) * kdim])
        tot_c = msum(sel, hi, lo)
        cums = [msum(tri, hi[gi * grp:(gi + 1) * grp], lo[gi * grp:(gi + 1) * grp])
                for gi in range(tm // grp)]
        sums.append((tot_c, cums))
    zvr = jnp.dot(h, win_ref[:, 2 * kdim:], preferred_element_type=F32)
    for (qb_ref, kb_ref, kd_ref, dec_ref, _), (tot_c, cums) in zip(outs, sums):
        dec_ref[0] = jnp.exp(tot_c[:nch])
        for gi, cum in enumerate(cums):
            rs = slice(gi * grp, (gi + 1) * grp)
            tot = jnp.concatenate(
                [jnp.broadcast_to(tot_c[c:c + 1], (GLA_CHUNK, kdim)) for c in range(gi * gch, (gi + 1) * gch)],
                axis=0)
            qb_ref[0, rs, :] = (q[rs] * jnp.exp(cum)).astype(BF16)
            kb_ref[0, rs, :] = (k[rs] * jnp.exp(-cum)).astype(BF16)
            kd_ref[0, rs, :] = (k[rs] * jnp.exp(tot - cum)).astype(BF16)
    v_ref[0] = zvr[:, :vdim].astype(BF16)
    r_ref[0] = zvr[:, vdim:].astype(BF16)


def _gla_proj(x, mod3, row_of_b, g, w_in, wg1, wg2, bg, kdim, vdim):
    b, l, d = x.shape
    tm = min(512, l)
    nc = tm // GLA_CHUNK
    tok_bf = lambda w: jax.ShapeDtypeStruct((b, l, w), BF16)
    dec = jax.ShapeDtypeStruct((b, l // GLA_CHUNK, kdim), F32)
    dec_spec = pl.BlockSpec((1, nc, kdim), lambda bb, i: (bb, i, 0))
    dir_shapes = [tok_bf(kdim), tok_bf(kdim), tok_bf(kdim), dec]
    dir_specs = [_tok_spec(tm, kdim)] * 3 + [dec_spec]
    return pl.pallas_call(
        functools.partial(_gla_proj_kernel, kdim=kdim, vdim=vdim), name="gla_proj",
        grid=(b, l // tm),
        in_specs=[_tok_spec(tm, d), _mod_spec(d, 0, row_of_b), _mod_spec(d, 1, row_of_b),
                  _full_spec((1, d)), _full_spec(w_in.shape), _full_spec(wg1.shape),
                  _full_spec(wg2.shape), _full_spec(bg.shape)],
        out_specs=dir_specs + dir_specs + [_tok_spec(tm, vdim), _tok_spec(tm, vdim)],
        out_shape=dir_shapes + dir_shapes + [tok_bf(vdim), tok_bf(vdim)],
        compiler_params=_params("parallel", "parallel"),
    )(x, mod3, mod3, g.reshape(1, d), w_in, wg1, wg2, bg)


def _gla_scan_kernel(qf_ref, kf_ref, df_ref, cf_ref, vf_ref, qr_ref, kr_ref, dr_ref, cr_ref, vr_ref,
                     s0f_ref, s0r_ref, of_ref, or_ref, sff_ref, sfr_ref, s_scr, *, nchunk, dk, dv):
    j = pl.program_id(1)

    @pl.when(j == 0)
    def _():
        s_scr[0] = s0f_ref[0]
        s_scr[1] = s0r_ref[0]

    row = lax.broadcasted_iota(jnp.int32, (GLA_CHUNK, GLA_CHUNK), 0)
    col = lax.broadcasted_iota(jnp.int32, (GLA_CHUNK, GLA_CHUNK), 1)
    dirs = ((qf_ref, kf_ref, df_ref, cf_ref, vf_ref, of_ref, col <= row),
            (qr_ref, kr_ref, dr_ref, cr_ref, vr_ref, or_ref, col >= row))
    chains = [(d, h) for d in range(2) for h in range(GLA_HEADS)]
    for step in range(nchunk):
        local = []
        for d, h in chains:
            qb_ref, kb_ref, kd_ref, _, v_ref, _, mask = dirs[d]
            c = step if d == 0 else nchunk - 1 - step
            rs = slice(c * GLA_CHUNK, (c + 1) * GLA_CHUNK)
            ks = slice(h * dk, (h + 1) * dk)
            v = v_ref[0, rs, h * dv:(h + 1) * dv]
            a = lax.dot_general(qb_ref[0, rs, ks], kb_ref[0, rs, ks], NT_DIMS, preferred_element_type=F32)
            u = lax.dot_general(v, kd_ref[0, rs, ks], TN_DIMS, preferred_element_type=F32)
            local.append((jnp.where(mask, a, 0.0).astype(BF16), u))
        for (d, h), (a, u) in zip(chains, local):
            qb_ref, _, _, dec_ref, v_ref, o_ref, _ = dirs[d]
            c = step if d == 0 else nchunk - 1 - step
            rs = slice(c * GLA_CHUNK, (c + 1) * GLA_CHUNK)
            ks = slice(h * dk, (h + 1) * dk)
            vs = slice(h * dv, (h + 1) * dv)
            s_t = s_scr[d, h]
            o = (lax.dot_general(qb_ref[0, rs, ks], s_t.astype(BF16), NT_DIMS, preferred_element_type=F32)
                 + jnp.dot(a, v_ref[0, rs, vs], preferred_element_type=F32))
            o_ref[0, rs, vs] = o.astype(BF16)
            s_scr[d, h] = dec_ref[0, c:c + 1, ks] * s_t + u

    @pl.when(j == pl.num_programs(1) - 1)
    def _():
        sff_ref[0] = s_scr[0]
        sfr_ref[0] = s_scr[1]


def _gla_scan(p, s0f, s0r):
    qf, kf, df, cf, qr, kr, dr, cr, v = p[:9]
    b, l, kdim = qf.shape
    vdim = v.shape[2]
    dk, dv = kdim // GLA_HEADS, vdim // GLA_HEADS
    tc = min(512, l)
    n = l // tc
    nchunk = tc // GLA_CHUNK
    fwd = lambda w: pl.BlockSpec((1, tc, w), lambda bb, j: (bb, j, 0))
    rev = lambda w: pl.BlockSpec((1, tc, w), lambda bb, j: (bb, n - 1 - j, 0))
    dec_f = pl.BlockSpec((1, nchunk, kdim), lambda bb, j: (bb, j, 0))
    dec_r = pl.BlockSpec((1, nchunk, kdim), lambda bb, j: (bb, n - 1 - j, 0))
    sspec = pl.BlockSpec((1, GLA_HEADS, dv, dk), lambda bb, j: (bb, 0, 0, 0))
    o_shape = jax.ShapeDtypeStruct((b, l, vdim), BF16)
    s_shape = jax.ShapeDtypeStruct((b, GLA_HEADS, dv, dk), F32)
    return pl.pallas_call(
        functools.partial(_gla_scan_kernel, nchunk=nchunk, dk=dk, dv=dv), name="gla_scan",
        grid=(b, n),
        in_specs=[fwd(kdim), fwd(kdim), fwd(kdim), dec_f, fwd(vdim),
                  rev(kdim), rev(kdim), rev(kdim), dec_r, rev(vdim), sspec, sspec],
        out_specs=[fwd(vdim), rev(vdim), sspec, sspec],
        out_shape=[o_shape, o_shape, s_shape, s_shape],
        scratch_shapes=[pltpu.VMEM((2, GLA_HEADS, dv, dk), F32)],
        compiler_params=_params("parallel", "arbitrary"),
    )(qf, kf, df, cf, v, qr, kr, dr, cr, v, s0f, s0r)


def _gla_finish_kernel(x_ref, of_ref, ob_ref, r_ref, gt_ref, gh_ref, wout_ref, o_ref):
    o = of_ref[0].astype(F32) + ob_ref[0].astype(F32)
    dv = gh_ref.shape[1]
    parts = []
    for hd in range(o.shape[1] // dv):
        oh = o[:, hd * dv:(hd + 1) * dv]
        ms = jnp.mean(oh * oh, axis=-1, keepdims=True)
        parts.append(oh * lax.rsqrt(ms + EPS) * gh_ref[...])
    on = jnp.concatenate(parts, axis=-1)
    r = r_ref[0].astype(F32)
    y = on * (r * _sigmoid(r))
    o_ref[0] = x_ref[0] + gt_ref[0] * jnp.dot(y.astype(BF16), wout_ref[...], preferred_element_type=F32)


def _gla_finish(x, o_f, o_b, r, mod3, row_of_b, g_head, w_out):
    b, l, d = x.shape
    vdim = r.shape[2]
    tm = min(512, l)
    return pl.pallas_call(
        _gla_finish_kernel, name="gla_finish",
        grid=(b, l // tm),
        in_specs=[_tok_spec(tm, d), _tok_spec(tm, vdim), _tok_spec(tm, vdim), _tok_spec(tm, vdim),
                  _mod_spec(d, 2, row_of_b), _full_spec((1, g_head.shape[0])), _full_spec(w_out.shape)],
        out_specs=_tok_spec(tm, d),
        out_shape=jax.ShapeDtypeStruct((b, l, d), F32),
        compiler_params=_params("parallel", "parallel"),
    )(x, o_f, o_b, r, mod3, g_head.reshape(1, -1), w_out)


def _gla_weights(w_in, w_g1, w_g2, b_g):
    rank, kdim = w_g2.shape[1], w_g2.shape[2]
    wg1 = jnp.concatenate([w_g1[0], w_g1[1]], axis=1).astype(BF16)
    wg2 = jnp.zeros((2 * rank, 2 * kdim), F32)
    wg2 = wg2.at[:rank, :kdim].set(w_g2[0]).at[rank:, kdim:].set(w_g2[1]).astype(BF16)
    bg = jnp.concatenate([b_g[0], b_g[1]])[None, :]
    return w_in.astype(BF16), wg1, wg2, bg


def _gla_layer(x_lat, x_ctx, mod3, nb, g_mix, w_in, w_g1, w_g2, b_g, g_head, w_out, need_ctx):
    kdim = w_g2.shape[2]
    vdim = w_out.shape[0]
    w_in, wg1, wg2, bg = _gla_weights(w_in, w_g1, w_g2, b_g)
    w_out = w_out.astype(BF16)
    lat_row, ctx_row = (lambda b: b), (lambda b: nb)
    pc = _gla_proj(x_ctx, mod3, ctx_row, g_mix, w_in, wg1, wg2, bg, kdim, vdim)
    pq = _gla_proj(x_lat, mod3, lat_row, g_mix, w_in, wg1, wg2, bg, kdim, vdim)
    s0 = jnp.zeros((nb, GLA_HEADS, vdim // GLA_HEADS, kdim // GLA_HEADS), F32)
    oc_f, oc_b, s_f, s_b = _gla_scan(pc, s0, s0)
    ol_f, ol_b, _, _ = _gla_scan(pq, s_f, s_b)
    x_lat = _gla_finish(x_lat, ol_f, ol_b, pq[9], mod3, lat_row, g_head, w_out)
    if need_ctx:
        x_ctx = _gla_finish(x_ctx, oc_f, oc_b, pc[9], mod3, ctx_row, g_head, w_out)
    return x_lat, x_ctx


def _conv_pw1_kernel(x_ref, sh_ref, sc_ref, g_ref, w_ref, b_ref, u_ref):
    d = x_ref.shape[2]
    h = _rms_mod(x_ref[0], g_ref[...], sc_ref[0], sh_ref[0]).astype(BF16)
    z = jnp.dot(h, w_ref[...], preferred_element_type=F32) + b_ref[...]
    u_ref[0] = z[:, :d] * _sigmoid(z[:, d:])


def _conv_pw1(x, mod3, row_of_b, g, w, bias):
    b, l, d = x.shape
    tm = min(512, l)
    return pl.pallas_call(
        _conv_pw1_kernel, name="conv_pw1",
        grid=(b, l // tm),
        in_specs=[_tok_spec(tm, d), _mod_spec(d, 0, row_of_b), _mod_spec(d, 1, row_of_b),
                  _full_spec((1, d)), _full_spec(w.shape), _full_spec((1, 2 * d))],
        out_specs=_tok_spec(tm, d),
        out_shape=jax.ShapeDtypeStruct((b, l, d), F32),
        compiler_params=_params("parallel", "parallel"),
    )(x, mod3, mod3, g.reshape(1, d), w, bias.reshape(1, 2 * d))


CONV_HALO = 16
CONV_RB = 64


def _conv_dw_kernel(x_ref, up_ref, uc_ref, un_ref, gt_ref, wdw_ref, bdw_ref, lng_ref, lnb_ref,
                    w2_ref, b2_ref, o_ref, ubuf, sbuf, cbuf, *, width):
    i = pl.program_id(1)
    tm, d = uc_ref.shape[1], uc_ref.shape[2]
    pad = width // 2
    ubuf[0:CONV_HALO, :] = jnp.where(i > 0, up_ref[0], 0.0)
    ubuf[CONV_HALO:CONV_HALO + tm, :] = uc_ref[0]
    ubuf[CONV_HALO + tm:, :] = jnp.where(i < pl.num_programs(1) - 1, un_ref[0], 0.0)
    srows = sbuf.shape[1]
    for s in range(1, 8):
        sbuf[s - 1] = ubuf[s:s + srows, :]
    off = CONV_HALO - pad
    for cb in range(d // 128):
        cs = slice(cb * 128, (cb + 1) * 128)
        w = wdw_ref[:, cs]
        for rb in range(tm // CONV_RB):
            acc = jnp.zeros((CONV_RB, 128), F32)
            for k in range(width):
                start = rb * CONV_RB + (off + k) // 8 * 8
                s = (off + k) % 8
                rows = ubuf[start:start + CONV_RB, cs] if s == 0 else sbuf[s - 1, start:start + CONV_RB, cs]
                acc = acc + w[k:k + 1, :] * rows
            cbuf[rb * CONV_RB:(rb + 1) * CONV_RB, cs] = acc
    u = cbuf[...] + bdw_ref[...]
    mu = jnp.mean(u, axis=-1, keepdims=True)
    uc = u - mu
    y = uc * lax.rsqrt(jnp.mean(uc * uc, axis=-1, keepdims=True) + EPS) * lng_ref[...] + lnb_ref[...]
    y = y * _sigmoid(y)
    y = jnp.dot(y.astype(BF16), w2_ref[...], preferred_element_type=F32) + b2_ref[...]
    o_ref[0] = x_ref[0] + gt_ref[0] * y


def _conv_dw(x, u, mod3, row_of_b, w_dw, b_dw, ln_g, ln_b, w2, b2):
    b, l, d = x.shape
    width = w_dw.shape[0]
    assert width // 2 <= CONV_HALO
    tm = min(256, l)
    nh = tm // CONV_HALO
    last = l // CONV_HALO - 1
    wpad = jnp.zeros((-(-width // 8) * 8, d), F32).at[:width].set(w_dw)
    vec = lambda a: a.reshape(1, d)
    return pl.pallas_call(
        functools.partial(_conv_dw_kernel, width=width), name="conv_dw",
        grid=(b, l // tm),
        in_specs=[_tok_spec(tm, d),
                  pl.BlockSpec((1, CONV_HALO, d), lambda bb, i: (bb, jnp.maximum(i * nh - 1, 0), 0)),
                  _tok_spec(tm, d),
                  pl.BlockSpec((1, CONV_HALO, d), lambda bb, i: (bb, jnp.minimum((i + 1) * nh, last), 0)),
                  _mod_spec(d, 2, row_of_b), _full_spec(wpad.shape), _full_spec((1, d)),
                  _full_spec((1, d)), _full_spec((1, d)), _full_spec(w2.shape), _full_spec((1, d))],
        out_specs=_tok_spec(tm, d),
        out_shape=jax.ShapeDtypeStruct((b, l, d), F32),
        scratch_shapes=[pltpu.VMEM((tm + 2 * CONV_HALO, d), F32),
                        pltpu.VMEM((7, tm + 2 * CONV_HALO - 8, d), F32),
                        pltpu.VMEM((tm, d), F32)],
        compiler_params=_params("parallel", "parallel"),
    )(x, u, u, u, mod3, wpad, vec(b_dw), vec(ln_g), vec(ln_b), w2, vec(b2))


def _conv_layer(x_lat, x_ctx, mod3, nb, g_mix, w_pw1, b_pw1, w_dw, b_dw, ln_g, ln_b, w_pw2, b_pw2, need_ctx):
    w1, w2 = w_pw1.astype(BF16), w_pw2.astype(BF16)
    streams = [(x_lat, lambda b: b)] + ([(x_ctx, lambda b: nb)] if need_ctx else [])
    outs = []
    for xs, row in streams:
        u = _conv_pw1(xs, mod3, row, g_mix, w1, b_pw1)
        outs.append(_conv_dw(xs, u, mod3, row, w_dw, b_dw, ln_g, ln_b, w2, b_pw2))
    return outs[0], (outs[1] if need_ctx else x_ctx)


def _attn_proj_kernel(x_ref, sh_ref, sc_ref, g_ref, w_ref, gq_ref, gk_ref, cos_ref, sin_ref,
                      q_ref, k_ref, vt_ref, *, qdim, kvdim):
    hd = ATTN_HEAD_DIM
    h = _rms_mod(x_ref[0], g_ref[...], sc_ref[0], sh_ref[0]).astype(BF16)
    z = jnp.dot(h, w_ref[...], preferred_element_type=F32)
    row = lax.broadcasted_iota(jnp.int32, (128, 128), 0)
    col = lax.broadcasted_iota(jnp.int32, (128, 128), 1)
    shift = hd.bit_length() - 1
    head_ones = ((row >> shift) == (col >> shift)).astype(BF16)
    lane = lax.broadcasted_iota(jnp.int32, (1, 128), 1)
    upper = (lane & (hd - 1)) >= (hd // 2)
    cos = cos_ref[...]
    sin = sin_ref[...]

    def norm_rope(blk, gain):
        ssq = jnp.dot((blk * blk).astype(BF16), head_ones, preferred_element_type=F32)
        n = blk * lax.rsqrt(ssq * (1.0 / hd) + EPS) * gain
        partner = jnp.where(upper, pltpu.roll(n, hd // 2, 1), pltpu.roll(n, 128 - hd // 2, 1))
        return n * cos + partner * sin

    tm = x_ref.shape[1]
    ones_rows = (lax.broadcasted_iota(jnp.int32, (VT_ROWS - hd, tm), 0) == 0).astype(BF16)
    for j in range(qdim // 128):
        blk = norm_rope(z[:, j * 128:(j + 1) * 128], gq_ref[...]) * Q_PRESCALE
        q_ref[0, 2 * j] = blk[:, :hd].astype(BF16)
        q_ref[0, 2 * j + 1] = blk[:, hd:].astype(BF16)
    for j in range(kvdim // 128):
        blk = norm_rope(z[:, qdim + j * 128:qdim + (j + 1) * 128], gk_ref[...])
        k_ref[0, 2 * j] = blk[:, :hd].astype(BF16)
        k_ref[0, 2 * j + 1] = blk[:, hd:].astype(BF16)
        vbt = z[:, qdim + kvdim + j * 128:qdim + kvdim + (j + 1) * 128].T
        for half in range(2):
            vt_ref[0, 2 * j + half, 0:hd, :] = vbt[half * hd:(half + 1) * hd].astype(BF16)
            vt_ref[0, 2 * j + half, hd:VT_ROWS, :] = ones_rows


def _attn_proj(x, mod3, row_of_b, g, w, g_q, g_k, cos, sin):
    b, l, d = x.shape
    hd = ATTN_HEAD_DIM
    kvdim = ATTN_KV_HEADS * hd
    qdim = w.shape[1] - 2 * kvdim
    tm = min(512, l)
    tile2 = lambda a: jnp.concatenate([a, a]).reshape(1, 2 * hd)
    head_spec = lambda nh: pl.BlockSpec((1, nh, tm, hd), lambda bb, i: (bb, 0, i, 0))
    head_shape = lambda nh: jax.ShapeDtypeStruct((b, nh, l, hd), BF16)
    return pl.pallas_call(
        functools.partial(_attn_proj_kernel, qdim=qdim, kvdim=kvdim), name="attn_proj",
        grid=(b, l // tm),
        in_specs=[_tok_spec(tm, d), _mod_spec(d, 0, row_of_b), _mod_spec(d, 1, row_of_b),
                  _full_spec((1, d)), _full_spec(w.shape), _full_spec((1, 2 * hd)), _full_spec((1, 2 * hd)),
                  pl.BlockSpec((tm, 128), lambda bb, i: (i, 0)), pl.BlockSpec((tm, 128), lambda bb, i: (i, 0))],
        out_specs=[head_spec(qdim // hd), head_spec(ATTN_KV_HEADS),
                   pl.BlockSpec((1, ATTN_KV_HEADS, VT_ROWS, tm), lambda bb, i: (bb, 0, 0, i))],
        out_shape=[head_shape(qdim // hd), head_shape(ATTN_KV_HEADS),
                   jax.ShapeDtypeStruct((b, ATTN_KV_HEADS, VT_ROWS, l), BF16)],
        compiler_params=_params("parallel", "parallel"),
    )(x, mod3, mod3, g.reshape(1, d), w, tile2(g_q), tile2(g_k), cos, sin)


def _flash_kernel(unshifted_ref, q_ref, k_ref, vt_ref, o_ref):
    grp, tq, hd = q_ref.shape[1], q_ref.shape[2], q_ref.shape[3]
    rows = grp * tq
    q = q_ref[0].reshape(rows, hd)
    nk, tk = vt_ref.shape[2], vt_ref.shape[4]

    def scores_t(c):
        start = c * tk if isinstance(c, int) else pl.multiple_of(c * tk, tk)
        return lax.dot_general(k_ref[0, 0, pl.ds(start, tk), :], q, NT_DIMS, preferred_element_type=F32)

    def finish(acc):
        o_t = acc[:hd] / acc[hd:hd + 1]
        o_ref[0] = jnp.concatenate([o_t[:, g * tq:(g + 1) * tq].T for g in range(grp)],
                                   axis=-1).astype(BF16)

    @pl.when(unshifted_ref[0] == 1)
    def _():
        def body(c, acc):
            p_t = jnp.exp2(scores_t(c)).astype(BF16)
            return acc + jnp.dot(vt_ref[0, 0, c], p_t, preferred_element_type=F32)

        finish(lax.fori_loop(0, nk, body, jnp.zeros((VT_ROWS, rows), F32), unroll=True))

    @pl.when(unshifted_ref[0] != 1)
    def _():
        def body(c, carry):
            m, acc = carry
            s_t = scores_t(c)
            m_new = jnp.maximum(m, jnp.max(s_t, axis=0, keepdims=True))
            p_t = jnp.exp2(s_t - m_new).astype(BF16)
            acc = jnp.exp2(m - m_new) * acc + jnp.dot(vt_ref[0, 0, c], p_t, preferred_element_type=F32)
            return m_new, acc

        init = (jnp.full((1, rows), -jnp.inf, F32), jnp.zeros((VT_ROWS, rows), F32))
        finish(lax.fori_loop(0, nk, body, init)[1])


def _flash(q, k, vt, unshifted):
    b, nq, l, hd = q.shape
    nkv, s = k.shape[1], k.shape[2]
    grp = nq // nkv
    tq = min(128, l)
    tk = 1280 if s % 1280 == 0 else 256
    assert s % tk == 0
    vt = vt.reshape(b, nkv, VT_ROWS, s // tk, tk).transpose(0, 1, 3, 2, 4)
    return pl.pallas_call(
        _flash_kernel, name="flash",
        grid_spec=pltpu.PrefetchScalarGridSpec(
            num_scalar_prefetch=1,
            grid=(b, nkv, l // tq),
            in_specs=[pl.BlockSpec((1, grp, tq, hd), lambda bb, h, i, *_: (bb, h, i, 0)),
                      pl.BlockSpec((1, 1, s, hd), lambda bb, h, i, *_: (bb, h, 0, 0)),
                      pl.BlockSpec((1, 1, s // tk, VT_ROWS, tk), lambda bb, h, i, *_: (bb, h, 0, 0, 0))],
            out_specs=pl.BlockSpec((1, tq, grp * hd), lambda bb, h, i, *_: (bb, i, h))),
        out_shape=jax.ShapeDtypeStruct((b, l, nq * hd), BF16),
        compiler_params=_params("parallel", "parallel", "parallel"),
    )(unshifted, q, k, vt)


def _attn_out_kernel(x_ref, o_ref_in, gt_ref, w_ref, o_ref):
    o_ref[0] = x_ref[0] + gt_ref[0] * jnp.dot(o_ref_in[0], w_ref[...], preferred_element_type=F32)


def _attn_out(x, o, mod3, row_of_b, w):
    b, l, d = x.shape
    tm = min(512, l)
    return pl.pallas_call(
        _attn_out_kernel, name="attn_out",
        grid=(b, l // tm),
        in_specs=[_tok_spec(tm, d), _tok_spec(tm, o.shape[2]), _mod_spec(d, 2, row_of_b), _full_spec(w.shape)],
        out_specs=_tok_spec(tm, d),
        out_shape=jax.ShapeDtypeStruct((b, l, d), F32),
        compiler_params=_params("parallel", "parallel"),
    )(x, o, mod3, w)


def _rope_tables(l):
    n = ATTN_HEAD_DIM // 4
    t = jnp.arange(l, dtype=jnp.int32)
    inv = ROPE_THETA ** (-jnp.arange(n, dtype=F32) / n)
    ang = jnp.concatenate([(t // GRID_W).astype(F32)[:, None] * inv,
                           (t % GRID_W).astype(F32)[:, None] * inv], axis=-1)
    cos, sin = jnp.cos(ang), jnp.sin(ang)
    cos_h = jnp.concatenate([cos, cos], axis=-1)
    sin_h = jnp.concatenate([-sin, sin], axis=-1)
    return jnp.concatenate([cos_h, cos_h], axis=-1), jnp.concatenate([sin_h, sin_h], axis=-1)


def _attn_layer(x_lat, x_ctx, mod3, nb, g_mix, w_qkv, g_q, g_k, w_out, need_ctx):
    l, lc = x_lat.shape[1], x_ctx.shape[1]
    w_qkv, w_out = w_qkv.astype(BF16), w_out.astype(BF16)
    lat_row, ctx_row = (lambda b: b), (lambda b: nb)
    cos, sin = _rope_tables(l)
    ql, kl, vl = _attn_proj(x_lat, mod3, lat_row, g_mix, w_qkv, g_q, g_k, cos, sin)
    qc, kc, vc = _attn_proj(x_ctx, mod3, ctx_row, g_mix, w_qkv, g_q, g_k,
                            jnp.ones((lc, 128), F32), jnp.zeros((lc, 128), F32))
    score_bound = ATTN_HEAD_DIM * Q_PRESCALE * jnp.max(jnp.abs(g_q)) * jnp.max(jnp.abs(g_k))
    unshifted = (score_bound <= UNSHIFTED_SCORE_LIMIT).astype(jnp.int32).reshape(1)
    o_lat = _flash(ql, jnp.concatenate([kc, kl], axis=2), jnp.concatenate([vc, vl], axis=3), unshifted)
    x_lat = _attn_out(x_lat, o_lat, mod3, lat_row, w_out)
    if need_ctx:
        x_ctx = _attn_out(x_ctx, _flash(qc, kc, vc, unshifted), mod3, ctx_row, w_out)
    return x_lat, x_ctx


def kernel(x, c, ctx, c_ctx, w_mod, b_mod, g_norm_mix, g_norm_mlp, w_mlp_in, w_mlp_out,
           gla_w_in, gla_w_g1, gla_w_g2, gla_b_g, gla_g_head, gla_w_out,
           conv_w_pw1, conv_b_pw1, conv_w_dw, conv_b_dw, conv_ln_g, conv_ln_b, conv_w_pw2, conv_b_pw2,
           attn_w_qkv, attn_g_q, attn_g_k, attn_w_out):
    nb, _, d = x.shape
    depth = w_mod.shape[0]
    mod = _modulation(c, c_ctx, w_mod, b_mod)
    x_lat, x_ctx = x, ctx
    for i in range(depth):
        last = i == depth - 1
        kind, j = i % 3, i // 3
        mod3 = mod[i].reshape(MOD_ROWS, 1, 6 * d)
        if kind == 0:
            x_lat, x_ctx = _gla_layer(x_lat, x_ctx, mod3, nb, g_norm_mix[i], gla_w_in[j], gla_w_g1[j],
                                      gla_w_g2[j], gla_b_g[j], gla_g_head[j], gla_w_out[j], not last)
        elif kind == 1:
            x_lat, x_ctx = _conv_layer(x_lat, x_ctx, mod3, nb, g_norm_mix[i], conv_w_pw1[j], conv_b_pw1[j],
                                       conv_w_dw[j], conv_b_dw[j], conv_ln_g[j], conv_ln_b[j],
                                       conv_w_pw2[j], conv_b_pw2[j], not last)
        else:
            x_lat, x_ctx = _attn_layer(x_lat, x_ctx, mod3, nb, g_norm_mix[i], attn_w_qkv[j], attn_g_q[j],
                                       attn_g_k[j], attn_w_out[j], not last)
        w_in, w_out = w_mlp_in[i].astype(BF16), w_mlp_out[i].astype(BF16)
        x_lat = _mlp(x_lat, mod3, lambda b: b, g_norm_mlp[i], w_in, w_out)
        if not last:
            x_ctx = _mlp(x_ctx, mod3, lambda b: nb, g_norm_mlp[i], w_in, w_out)
    return x_lat
```
